```python
import math
import jax, jax.numpy as jnp
from jax import lax
import numpy as np

D_MODEL = 4096
BATCH = 4
SEQ = 2048
DEPTH = 1
DEC_BATCH = 32
DEC_SEQ = 8
PAST_LEN = 8192
PAGE_SIZE = 128

MOBA_HEADS = 16
MOBA_KV_HEADS = 4
MOBA_HD = 128
MOBA_GROUP = MOBA_HEADS // MOBA_KV_HEADS
MOBA_BLOCK = 256
MOBA_TOPK = 3
MOBA_QUERY_CHUNK = 16
D_INNER = 4096
SSD_HEADDIM = 64
SSD_HEADS = D_INNER // SSD_HEADDIM
SSD_GROUPS = 8
SSD_STATE = 128
SSD_CHUNK = 256
CONV_W = 4
CONV_CH = D_INNER + 2 * SSD_GROUPS * SSD_STATE
MEM_TOKENS = 256
MEM_HEADS = 4
MEM_HD = 512
N_BRANCH = 3
N_EXPERTS = 32
TOP_K = 4
D_FF = 4096
SWIGLU_LIMIT = 7.0
SWIGLU_ALPHA = 1.702
EPS = 1e-6
MOBA_Q_W = MOBA_HEADS * MOBA_HD
MOBA_KV_W = MOBA_KV_HEADS * MOBA_HD
MEM_W = MEM_HEADS * MEM_HD
IN_WIDTHS = (MOBA_Q_W, MOBA_KV_W, MOBA_KV_W, D_INNER, CONV_CH, SSD_HEADS, MEM_W, N_BRANCH * D_MODEL)
IN_COLS = sum(IN_WIDTHS)

kernel_name = 'hybrid_moba_ssd_memxattn_moe_step'


def rmsnorm(x, g):
    xf = x.astype(jnp.float32)
    xf = xf * lax.rsqrt(jnp.mean(xf * xf, axis=-1, keepdims=True) + EPS)
    return xf.astype(x.dtype) * g


def split_in_proj(proj):
    pts = np.cumsum(IN_WIDTHS)[:-1].tolist()
    return jnp.split(proj, pts, axis=-1)


def causal_conv(u, prev, w, b):
    L = u.shape[1]
    up = jnp.concatenate([prev.astype(u.dtype), u], axis=1)
    out = b + up[:, 0:L] * w[0]
    for tap in range(1, CONV_W):
        out = out + up[:, tap:tap + L] * w[tap]
    return out, up[:, L:]


def ssd_chunked(x, dt, a, bm, cm, s0):
    b, L, H, P = x.shape
    G, N = bm.shape[2], bm.shape[3]
    R = H // G
    Q = math.gcd(L, SSD_CHUNK)
    c = L // Q
    xr = x.reshape(b, c, Q, G, R, P)
    dtr = dt.reshape(b, c, Q, G, R)
    br = bm.reshape(b, c, Q, G, N)
    cr = cm.reshape(b, c, Q, G, N)
    acs = jnp.cumsum(dtr * a.reshape(G, R), axis=2)
    causal = jnp.tril(jnp.ones((Q, Q), bool))[:, :, None, None]
    seg = acs[:, :, :, None] - acs[:, :, None, :]
    decay = jnp.exp(jnp.where(causal, seg, -jnp.inf))
    cb = jnp.einsum('bcign,bcjgn->bcijg', cr, br)
    y_diag = jnp.einsum('bcijgr,bcjgrp->bcigrp', cb[..., None] * decay * dtr[:, :, None], xr)
    to_end = jnp.exp(acs[:, :, -1:] - acs) * dtr
    chunk_states = jnp.einsum('bcjgn,bcjgr,bcjgrp->bcgrpn', br, to_end, xr)
    chunk_decay = jnp.exp(acs[:, :, -1])

    def step(s, inp):
        dec, st = inp
        return s * dec[..., None, None] + st, s

    s_fin, s_in = lax.scan(step, s0.reshape(b, G, R, P, N),
                           (jnp.moveaxis(chunk_decay, 1, 0), jnp.moveaxis(chunk_states, 1, 0)))
    s_in = jnp.moveaxis(s_in, 0, 1)
    y_off = jnp.einsum('bcign,bcigr,bcgrpn->bcigrp', cr, jnp.exp(acs), s_in)
    return (y_diag + y_off).reshape(b, L, H, P), s_fin.reshape(b, H, P, N)


def moba_sequence(q, k, v, q_pos):
    T, Hkv, G, d = q.shape
    Lk = k.shape[0]
    nb = -(-Lk // MOBA_BLOCK)
    pad = nb * MOBA_BLOCK - Lk
    kb = jnp.pad(k, ((0, pad), (0, 0), (0, 0))).reshape(nb, MOBA_BLOCK, Hkv, d).transpose(2, 0, 1, 3)
    vb = jnp.pad(v, ((0, pad), (0, 0), (0, 0))).reshape(nb, MOBA_BLOCK, Hkv, d).transpose(2, 0, 1, 3)
    kmean = jnp.mean(kb.astype(jnp.float32), axis=2)
    own = q_pos // MOBA_BLOCK
    own4 = own[:, None, None, None]
    s = jnp.einsum('tkgd,knd->tkgn', q.astype(jnp.float32), kmean)
    s = jnp.where(jnp.arange(nb)[None, None, None, :] < own4, s, -jnp.inf)
    _, idx = lax.top_k(s, min(MOBA_TOPK, nb))
    idx = idx.astype(jnp.int32)
    blocks = jnp.concatenate([idx, jnp.broadcast_to(own4, (T, Hkv, G, 1))], axis=-1)
    slot_ok = jnp.concatenate([idx < own4, jnp.ones((T, Hkv, G, 1), bool)], axis=-1)
    qc = math.gcd(T, MOBA_QUERY_CHUNK)
    nc = T // qc
    kv_head = jnp.arange(Hkv)[None, :, None, None]
    offs = jnp.arange(MOBA_BLOCK, dtype=jnp.int32)

    def attend(args):
        qq, bl, ok, pos = args
        kg = kb[kv_head, bl]
        vg = vb[kv_head, bl]
        kpos = bl[..., None] * MOBA_BLOCK + offs
        mask = ok[..., None] & (kpos <= pos[:, None, None, None, None])
        logit = jnp.einsum('ckgd,ckgsjd->ckgsj', qq, kg).astype(jnp.float32)
        logit = jnp.where(mask, logit, -jnp.inf)
        shp = logit.shape
        w = jax.nn.softmax(logit.reshape(shp[:3] + (-1,)), axis=-1).reshape(shp)
        return jnp.einsum('ckgsj,ckgsjd->ckgd', w.astype(vg.dtype), vg)

    out = lax.map(attend, (q.reshape(nc, qc, Hkv, G, d), blocks.reshape(nc, qc, Hkv, G, -1),
                           slot_ok.reshape(nc, qc, Hkv, G, -1), q_pos.reshape(nc, qc)))
    return out.reshape(T, Hkv, G, d)


def memory_kv(mem, p):
    bm, m, _ = mem.shape
    kv = rmsnorm(mem, p['mem_norm_g']) @ p['w_mem_kv']
    k, v = jnp.split(kv, 2, axis=-1)
    k = rmsnorm(k.reshape(bm, m, MEM_HEADS, MEM_HD), p['mem_k_norm_g'])
    return k, v.reshape(bm, m, MEM_HEADS, MEM_HD)


def memory_attend(q, mk, mv):
    s = jnp.einsum('bthd,bmhd->bhtm', q, mk).astype(jnp.float32)
    w = jax.nn.softmax(s, axis=-1).astype(mv.dtype)
    return jnp.einsum('bhtm,bmhd->bthd', w, mv)


def moe_ffn(h, p):
    shp = h.shape
    t = h.reshape(-1, D_MODEL)
    logits = (t @ p['w_router'] + p['b_router']).astype(jnp.float32)
    top_val, top_idx = lax.top_k(logits, TOP_K)
    probs = jax.nn.softmax(top_val, axis=-1)
    combine = jnp.einsum('nk,nke->ne', probs, jax.nn.one_hot(top_idx, N_EXPERTS, dtype=jnp.float32)).astype(t.dtype)
    out = jnp.zeros_like(t)
    for e in range(N_EXPERTS):
        g = jnp.minimum(t @ p['w_exp_gate'][e] + p['b_exp_gate'][e], SWIGLU_LIMIT)
        u = jnp.clip(t @ p['w_exp_up'][e] + p['b_exp_up'][e], -SWIGLU_LIMIT, SWIGLU_LIMIT)
        act = (u + 1.0) * (g * jax.nn.sigmoid(SWIGLU_ALPHA * g))
        out = out + combine[:, e:e + 1] * (act @ p['w_exp_down'][e] + p['b_exp_down'][e])
    return out.reshape(shp)


def trunk_layer(x, past_k, past_v, ssm0, conv0, mem_k, mem_v, p):
    b, T, _ = x.shape
    P = past_k.shape[1]
    h = rmsnorm(x, p['norm_mix_g'])
    q, k_new, v_new, z, xbc_raw, dt_raw, q_mem, gate_logit = split_in_proj(h @ p['w_in'])
    q = rmsnorm(q.reshape(b, T, MOBA_KV_HEADS, MOBA_GROUP, MOBA_HD), p['moba_q_norm_g']) * (MOBA_HD ** -0.5)
    k_new = rmsnorm(k_new.reshape(b, T, MOBA_KV_HEADS, MOBA_HD), p['moba_k_norm_g'])
    v_new = v_new.reshape(b, T, MOBA_KV_HEADS, MOBA_HD)
    k_all = jnp.concatenate([past_k.astype(k_new.dtype), k_new], axis=1)
    v_all = jnp.concatenate([past_v.astype(v_new.dtype), v_new], axis=1)
    q_pos = P + jnp.arange(T, dtype=jnp.int32)
    attn = lax.map(lambda a: moba_sequence(a[0], a[1], a[2], q_pos), (q, k_all, v_all))
    out_a = attn.reshape(b, T, MOBA_Q_W) @ p['w_br_moba']
    xbc, conv_new = causal_conv(xbc_raw, conv0, p['conv_w'], p['conv_b'])
    xbc = jax.nn.silu(xbc).astype(jnp.float32)
    xs, bm, cm = jnp.split(xbc, [D_INNER, D_INNER + SSD_GROUPS * SSD_STATE], axis=-1)
    xh = xs.reshape(b, T, SSD_HEADS, SSD_HEADDIM)
    dt = jax.nn.softplus(dt_raw.astype(jnp.float32) + p['dt_bias'])
    a = -jnp.exp(p['a_log'].astype(jnp.float32))
    y, ssm_new = ssd_chunked(xh, dt, a, bm.reshape(b, T, SSD_GROUPS, SSD_STATE),
                             cm.reshape(b, T, SSD_GROUPS, SSD_STATE), ssm0.astype(jnp.float32))
    y = (y + p['d_skip'][:, None] * xh).reshape(b, T, D_INNER) * jax.nn.silu(z.astype(jnp.float32))
    y = rmsnorm(y.reshape(b, T, SSD_GROUPS, D_INNER // SSD_GROUPS), p['ssd_norm_g'].reshape(SSD_GROUPS, -1))
    out_b = y.reshape(b, T, D_INNER).astype(x.dtype) @ p['w_br_ssd']
    qm = rmsnorm(q_mem.reshape(b, T, MEM_HEADS, MEM_HD), p['mem_q_norm_g']) * (MEM_HD ** -0.5)
    out_c = memory_attend(qm, mem_k.astype(qm.dtype), mem_v.astype(qm.dtype)).reshape(b, T, MEM_W) @ p['w_br_mem']
    gates = jax.nn.sigmoid(gate_logit + p['b_branch_gate']).reshape(b, T, N_BRANCH, D_MODEL)
    merged = gates[:, :, 0] * out_a + gates[:, :, 1] * out_b + gates[:, :, 2] * out_c
    x = x + merged @ p['w_out']
    x = x + moe_ffn(rmsnorm(x, p['norm_ffn_g']), p)
    return x, k_new, v_new, ssm_new, conv_new


def setup_inputs(seed: int = 0) -> dict:
    key = jax.random.key(seed)
    k = jax.random.split(key, 40)
    f32 = jnp.float32
    n_pages = PAST_LEN // PAGE_SIZE
    n_pool = (DEC_BATCH * n_pages * 5 + 3) // 4

    def nrm(i, shape, scale):
        return jax.random.normal(k[i], shape, f32) * scale

    def gain(i, n):
        return 1.0 + 0.02 * jax.random.normal(k[i], (DEPTH, n), f32)

    page_table = jax.random.permutation(k[5], n_pool)[:DEC_BATCH * n_pages].reshape(DEC_BATCH, n_pages).astype(jnp.int32)
    dt0 = jnp.exp(jax.random.uniform(k[17], (DEPTH, SSD_HEADS), f32, math.log(1e-3), math.log(1e-1)))
    dt_bias = dt0 + jnp.log(-jnp.expm1(-dt0))
    a_log = jnp.log(jax.random.uniform(k[18], (DEPTH, SSD_HEADS), f32, 1.0, 16.0))
    return {
        'x_prompt': nrm(0, (BATCH, SEQ, D_MODEL), 1.0),
        'x_sample': nrm(1, (DEC_BATCH, DEC_SEQ, D_MODEL), 1.0),
        'mem_prompt': nrm(2, (BATCH, MEM_TOKENS, D_MODEL), 1.0),
        'cache_k': nrm(3, (DEPTH, n_pool, PAGE_SIZE, MOBA_KV_HEADS, MOBA_HD), 1.0),
        'cache_v': nrm(4, (DEPTH, n_pool, PAGE_SIZE, MOBA_KV_HEADS, MOBA_HD), 1.0),
        'page_table': page_table,
        'state_ssm': nrm(6, (DEPTH, DEC_BATCH, SSD_HEADS, SSD_HEADDIM, SSD_STATE), 0.1),
        'state_conv': nrm(7, (DEPTH, DEC_BATCH, CONV_W - 1, CONV_CH), 1.0),
        'cache_mem_k': nrm(8, (DEPTH, DEC_BATCH, MEM_TOKENS, MEM_HEADS, MEM_HD), 1.0),
        'cache_mem_v': nrm(9, (DEPTH, DEC_BATCH, MEM_TOKENS, MEM_HEADS, MEM_HD), 1.0),
        'norm_mix_g': gain(10, D_MODEL),
        'w_in': nrm(11, (DEPTH, D_MODEL, IN_COLS), D_MODEL ** -0.5),
        'b_branch_gate': nrm(12, (DEPTH, N_BRANCH * D_MODEL), 0.02),
        'moba_q_norm_g': gain(13, MOBA_HD),
        'moba_k_norm_g': gain(14, MOBA_HD),
        'conv_w': nrm(15, (DEPTH, CONV_W, CONV_CH), CONV_W ** -0.5),
        'conv_b': nrm(16, (DEPTH, CONV_CH), 0.02),
        'dt_bias': dt_bias,
        'a_log': a_log,
        'd_skip': 1.0 + nrm(19, (DEPTH, SSD_HEADS), 0.1),
        'ssd_norm_g': gain(20, D_INNER),
        'mem_norm_g': gain(21, D_MODEL),
        'w_mem_kv': nrm(22, (DEPTH, D_MODEL, 2 * MEM_W), D_MODEL ** -0.5),
        'mem_q_norm_g': gain(23, MEM_HD),
        'mem_k_norm_g': gain(24, MEM_HD),
        'w_br_moba': nrm(25, (DEPTH, MOBA_Q_W, D_MODEL), MOBA_Q_W ** -0.5),
        'w_br_ssd': nrm(26, (DEPTH, D_INNER, D_MODEL), D_INNER ** -0.5),
        'w_br_mem': nrm(27, (DEPTH, MEM_W, D_MODEL), MEM_W ** -0.5),
        'w_out': nrm(28, (DEPTH, D_MODEL, D_MODEL), D_MODEL ** -0.5),
        'norm_ffn_g': gain(29, D_MODEL),
        'w_router': nrm(30, (DEPTH, D_MODEL, N_EXPERTS), D_MODEL ** -0.5),
        'b_router': nrm(31, (DEPTH, N_EXPERTS), 0.01),
        'w_exp_gate': nrm(32, (DEPTH, N_EXPERTS, D_MODEL, D_FF), D_MODEL ** -0.5),
        'b_exp_gate': nrm(33, (DEPTH, N_EXPERTS, D_FF), 0.02),
        'w_exp_up': nrm(34, (DEPTH, N_EXPERTS, D_MODEL, D_FF), D_MODEL ** -0.5),
        'b_exp_up': nrm(35, (DEPTH, N_EXPERTS, D_FF), 0.02),
        'w_exp_down': nrm(36, (DEPTH, N_EXPERTS, D_FF, D_MODEL), D_FF ** -0.5),
        'b_exp_down': nrm(37, (DEPTH, N_EXPERTS, D_MODEL), 0.02),
    }


def reference(x_prompt, x_sample, mem_prompt, cache_k, cache_v, page_table, state_ssm, state_conv,
              cache_mem_k, cache_mem_v, norm_mix_g, w_in, b_branch_gate, moba_q_norm_g, moba_k_norm_g,
              conv_w, conv_b, dt_bias, a_log, d_skip, ssd_norm_g, mem_norm_g, w_mem_kv, mem_q_norm_g,
              mem_k_norm_g, w_br_moba, w_br_ssd, w_br_mem, w_out, norm_ffn_g, w_router, b_router,
              w_exp_gate, b_exp_gate, w_exp_up, b_exp_up, w_exp_down, b_exp_down):
    n_pr = x_prompt.shape[0]
    n_dec = x_sample.shape[0]
    xp = x_prompt
    xs = x_sample
    kp_l, vp_l, ks_l, vs_l = [], [], [], []
    sp_l, cp_l, ss_l, cs_l = [], [], [], []
    mk_l, mv_l = [], []
    for l in range(DEPTH):
        p = {
            'norm_mix_g': norm_mix_g[l], 'w_in': w_in[l], 'b_branch_gate': b_branch_gate[l],
            'moba_q_norm_g': moba_q_norm_g[l], 'moba_k_norm_g': moba_k_norm_g[l],
            'conv_w': conv_w[l], 'conv_b': conv_b[l], 'dt_bias': dt_bias[l], 'a_log': a_log[l],
            'd_skip': d_skip[l], 'ssd_norm_g': ssd_norm_g[l],
            'mem_norm_g': mem_norm_g[l], 'w_mem_kv': w_mem_kv[l],
            'mem_q_norm_g': mem_q_norm_g[l], 'mem_k_norm_g': mem_k_norm_g[l],
            'w_br_moba': w_br_moba[l], 'w_br_ssd': w_br_ssd[l], 'w_br_mem': w_br_mem[l], 'w_out': w_out[l],
            'norm_ffn_g': norm_ffn_g[l], 'w_router': w_router[l], 'b_router': b_router[l],
            'w_exp_gate': w_exp_gate[l], 'b_exp_gate': b_exp_gate[l], 'w_exp_up': w_exp_up[l],
            'b_exp_up': b_exp_up[l], 'w_exp_down': w_exp_down[l], 'b_exp_down': b_exp_down[l],
        }
        mk, mv = memory_kv(mem_prompt, p)
        empty = jnp.zeros((n_pr, 0, MOBA_KV_HEADS, MOBA_HD), xp.dtype)
        ssm_zero = jnp.zeros((n_pr, SSD_HEADS, SSD_HEADDIM, SSD_STATE), jnp.float32)
        conv_zero = jnp.zeros((n_pr, CONV_W - 1, CONV_CH), xp.dtype)
        xp, kp, vp, sp, cp = trunk_layer(xp, empty, empty, ssm_zero, conv_zero, mk, mv, p)
        past_k = cache_k[l][page_table].reshape(n_dec, -1, MOBA_KV_HEADS, MOBA_HD)
        past_v = cache_v[l][page_table].reshape(n_dec, -1, MOBA_KV_HEADS, MOBA_HD)
        xs, ks, vs, ss, cs = trunk_layer(xs, past_k, past_v, state_ssm[l], state_conv[l],
                                         cache_mem_k[l], cache_mem_v[l], p)
        kp_l.append(kp); vp_l.append(vp); ks_l.append(ks); vs_l.append(vs)
        sp_l.append(sp); cp_l.append(cp); ss_l.append(ss); cs_l.append(cs)
        mk_l.append(mk); mv_l.append(mv)
    k_prompt = jnp.stack(kp_l)
    v_prompt = jnp.stack(vp_l)
    k_sample = jnp.stack(ks_l)
    v_sample = jnp.stack(vs_l)
    ssm_prompt = jnp.stack(sp_l)
    conv_prompt = jnp.stack(cp_l)
    ssm_sample = jnp.stack(ss_l)
    conv_sample = jnp.stack(cs_l)
    mem_k_prompt = jnp.stack(mk_l)
    mem_v_prompt = jnp.stack(mv_l)
    return (xp, xs, k_prompt, v_prompt, k_sample, v_sample, ssm_prompt, conv_prompt, ssm_sample, conv_sample, mem_k_prompt, mem_v_prompt)
```

```python
import functools

import jax
import jax.numpy as jnp
from jax import lax
from jax.experimental import pallas as pl
from jax.experimental.pallas import tpu as pltpu

F32 = jnp.float32
BF16 = jnp.bfloat16
HI = lax.Precision.HIGHEST

D_MODEL = 4096
EPS = 1e-6
NEG = -1e30

MOBA_HEADS = 16
MOBA_KV = 4
MOBA_GROUP = 4
MOBA_HD = 128
MOBA_BLOCK = 256
MOBA_TOPK = 3
PAGE = 128
D_INNER = 4096
SSD_HEADS = 64
SSD_HD = 64
SSD_GROUPS = 8
SSD_STATE = 128
SSD_GW = D_INNER // SSD_GROUPS
CONV_CH = D_INNER + 2 * SSD_GROUPS * SSD_STATE
MEM_TOKENS = 256
MEM_HEADS = 4
MEM_HD = 512
N_EXPERTS = 32
TOP_K = 4
D_FF = 4096
SWIGLU_LIMIT = 7.0
SWIGLU_ALPHA = 1.702

C_Q = 0
C_K = 2048
C_V = 2560
C_Z = 3072
C_XBC = 7168
C_DT = 13312
C_QM = 13824
C_GATE = 15872
PROJ_W = 28160
W_IN_SPLIT = 13376

V7X_VMEM_LIMIT = 56 * 1024 * 1024

_NT = (((1,), (1,)), ((), ()))
_TN = (((0,), (0,)), ((), ()))


def _cp(*sem):
    return pltpu.CompilerParams(dimension_semantics=sem, vmem_limit_bytes=V7X_VMEM_LIMIT)


def _rmsnorm_kernel(x_ref, g_ref, o_ref):
    x = x_ref[...]
    ms = jnp.mean(x * x, axis=-1, keepdims=True)
    o_ref[...] = ((x * lax.rsqrt(ms + EPS)) * g_ref[...]).astype(o_ref.dtype)


def _rmsnorm(x, g, out_dtype, tr=256):
    m, d = x.shape
    return pl.pallas_call(
        _rmsnorm_kernel,
        grid=(m // tr,),
        in_specs=[pl.BlockSpec((tr, d), lambda i: (i, 0)), pl.BlockSpec((1, d), lambda i: (0, 0))],
        out_specs=pl.BlockSpec((tr, d), lambda i: (i, 0)),
        out_shape=jax.ShapeDtypeStruct((m, d), out_dtype),
        compiler_params=_cp("parallel"),
        name="rmsnorm",
    )(x, g.reshape(1, d))


def _head_norm_kernel(x_ref, g_ref, o_ref, *, hd, scale):
    x = x_ref[...]
    g = g_ref[...]
    for s in range(x.shape[1] // hd):
        xs = x[:, s * hd:(s + 1) * hd]
        ms = jnp.mean(xs * xs, axis=-1, keepdims=True)
        y = (xs * lax.rsqrt(ms + EPS)) * g[:, s * hd:(s + 1) * hd]
        if scale != 1.0:
            y = y * scale
        o_ref[:, s * hd:(s + 1) * hd] = y


def _head_norm(src, col_off, width, hd, gain, scale, tr=264):
    rows = src.shape[0]
    cb = col_off // 512
    return pl.pallas_call(
        functools.partial(_head_norm_kernel, hd=hd, scale=scale),
        grid=(rows // tr, width // 512),
        in_specs=[pl.BlockSpec((tr, 512), lambda i, j: (i, cb + j)),
                  pl.BlockSpec((1, 512), lambda i, j: (0, 0))],
        out_specs=pl.BlockSpec((tr, 512), lambda i, j: (i, j)),
        out_shape=jax.ShapeDtypeStruct((rows, width), F32),
        compiler_params=_cp("parallel", "parallel"),
        name="head_norm",
    )(src, jnp.tile(gain, 512 // hd).reshape(1, 512))


def _mm_kernel(a_ref, w_ref, o_ref):
    o_ref[...] = jnp.dot(a_ref[...], w_ref[...], preferred_element_type=F32).astype(o_ref.dtype)


def _mm_res_kernel(a_ref, w_ref, r_ref, o_ref):
    o_ref[...] = r_ref[...] + jnp.dot(a_ref[...], w_ref[...], preferred_element_type=F32)


def _matmul(a, w, tm, tn, out_dtype=F32, residual=None, name="matmul"):
    m, k = a.shape
    n = w.shape[1]
    in_specs = [pl.BlockSpec((tm, k), lambda i, j: (i, 0)), pl.BlockSpec((k, tn), lambda i, j: (0, j))]
    args = [a, w]
    body = _mm_kernel
    if residual is not None:
        in_specs.append(pl.BlockSpec((tm, tn), lambda i, j: (i, j)))
        args.append(residual)
        body = _mm_res_kernel
    return pl.pallas_call(
        body,
        grid=(m // tm, n // tn),
        in_specs=in_specs,
        out_specs=pl.BlockSpec((tm, tn), lambda i, j: (i, j)),
        out_shape=jax.ShapeDtypeStruct((m, n), out_dtype),
        compiler_params=_cp("parallel", "parallel"),
        name=name,
    )(*args)


def _topk_mask(s, valid, lane, k):
    lanef = lane.astype(F32)
    s = jnp.where(valid, s, -jnp.inf)
    sel = jnp.zeros(s.shape, F32)
    for _ in range(k):
        m = jnp.max(s, axis=-1, keepdims=True)
        idx = jnp.min(jnp.where(s == m, lanef, 128.0), axis=-1, keepdims=True)
        hit = lanef == idx
        sel = jnp.where(hit, 1.0, sel)
        s = jnp.where(hit, -jnp.inf, s)
    return jnp.where(valid, sel, 0.0)


def _moba_prompt_kernel(q_ref, k_ref, v_ref, o_ref, kmean_ref, acc_ref):
    qb = pl.program_id(2)
    blk = MOBA_BLOCK
    grp = MOBA_GROUP
    rows = grp * blk
    nblk = k_ref.shape[0] // blk

    @pl.when(qb == 0)
    def _():
        kmean_ref[...] = jnp.zeros_like(kmean_ref)
        for n in range(nblk):
            kmean_ref[n:n + 1, :] = jnp.sum(k_ref[n * blk:(n + 1) * blk, :], axis=0, keepdims=True) * (1.0 / blk)

    q4 = q_ref[...]
    q = jnp.concatenate([q4[:, g * MOBA_HD:(g + 1) * MOBA_HD] for g in range(grp)], axis=0)
    scores = lax.dot_general(q, kmean_ref[...], _NT, precision=HI, preferred_element_type=F32)
    lane = lax.broadcasted_iota(jnp.int32, (rows, 128), 1)
    sel = _topk_mask(scores, lane < qb, lane, MOBA_TOPK)
    q16 = q.astype(BF16)

    own = pl.multiple_of(qb * blk, blk)
    k_own = k_ref[pl.ds(own, blk), :].astype(BF16)
    v_own = v_ref[pl.ds(own, blk), :].astype(BF16)
    logit = lax.dot_general(q16, k_own, _NT, preferred_element_type=F32)
    t_row = lax.broadcasted_iota(jnp.int32, (rows, blk), 0) & (blk - 1)
    j_col = lax.broadcasted_iota(jnp.int32, (rows, blk), 1)
    logit = jnp.where(j_col <= t_row, logit, NEG)
    m0 = jnp.max(logit, axis=-1, keepdims=True)
    p0 = jnp.exp(logit - m0)
    l0 = jnp.sum(p0, axis=-1, keepdims=True)
    acc_ref[...] = jnp.dot(p0.astype(BF16), v_own, preferred_element_type=F32)

    def body(n, carry):
        m, l = carry
        gate = jnp.max(jnp.where(lane == n, sel, 0.0), axis=-1, keepdims=True) > 0.5
        off = pl.multiple_of(n * blk, blk)
        kb = k_ref[pl.ds(off, blk), :].astype(BF16)
        vb = v_ref[pl.ds(off, blk), :].astype(BF16)
        lg = lax.dot_general(q16, kb, _NT, preferred_element_type=F32)
        lg = jnp.where(gate, lg, NEG)
        m_new = jnp.maximum(m, jnp.max(lg, axis=-1, keepdims=True))
        alpha = jnp.exp(m - m_new)
        p = jnp.exp(lg - m_new)
        l = alpha * l + jnp.sum(p, axis=-1, keepdims=True)
        acc_ref[...] = alpha * acc_ref[...] + jnp.dot(p.astype(BF16), vb, preferred_element_type=F32)
        return m_new, l

    _, l = lax.fori_loop(0, qb, body, (m0, l0))
    out = acc_ref[...] / l
    for g in range(grp):
        o_ref[:, g * MOBA_HD:(g + 1) * MOBA_HD] = out[g * blk:(g + 1) * blk, :].astype(o_ref.dtype)


def _moba_prompt(qn, kn, proj, nb, seq):
    nqb = seq // MOBA_BLOCK
    gw = MOBA_GROUP * MOBA_HD
    vcb = C_V // MOBA_HD
    return pl.pallas_call(
        _moba_prompt_kernel,
        grid=(nb, MOBA_KV, nqb),
        in_specs=[pl.BlockSpec((MOBA_BLOCK, gw), lambda b, h, i: (b * nqb + i, h)),
                  pl.BlockSpec((seq, MOBA_HD), lambda b, h, i: (b, h)),
                  pl.BlockSpec((seq, MOBA_HD), lambda b, h, i: (b, vcb + h))],
        out_specs=pl.BlockSpec((MOBA_BLOCK, gw), lambda b, h, i: (b * nqb + i, h)),
        out_shape=jax.ShapeDtypeStruct((nb * seq, MOBA_HEADS * MOBA_HD), BF16),
        scratch_shapes=[pltpu.VMEM((128, MOBA_HD), F32),
                        pltpu.VMEM((MOBA_GROUP * MOBA_BLOCK, MOBA_HD), F32)],
        compiler_params=_cp("parallel", "parallel", "arbitrary"),
        name="moba_prompt",
    )(qn, kn, proj)


def _kmean_kernel(pt_ref, ka_ref, kb_ref, o_ref):
    s = jnp.sum(ka_ref[...], axis=0, keepdims=True) + jnp.sum(kb_ref[...], axis=0, keepdims=True)
    o_ref[...] = s * (1.0 / MOBA_BLOCK)


def _moba_sample_kernel(pt_ref, q_ref, kn_ref, vn_ref, km_ref, ka_ref, kb_ref, va_ref, vb_ref, o_ref,
                        qbd_ref, pad_ref, sel_ref, m_ref, l_ref, acc_ref):
    n = pl.program_id(1)
    nblk = pl.num_programs(1)
    nq = q_ref.shape[0]
    rows = MOBA_HEADS * nq
    kvw = MOBA_KV * MOBA_HD
    lane = lax.broadcasted_iota(jnp.int32, (rows, 128), 1)

    @pl.when(n == 0)
    def _():
        qbd_ref[...] = jnp.zeros_like(qbd_ref)
        for h in range(MOBA_HEADS):
            kv = h // MOBA_GROUP
            qbd_ref[h * nq:(h + 1) * nq, kv * MOBA_HD:(kv + 1) * MOBA_HD] = q_ref[:, h * MOBA_HD:(h + 1) * MOBA_HD]
        qbd = qbd_ref[...]
        pad_ref[...] = jnp.zeros_like(pad_ref)
        pad_ref[0:km_ref.shape[0], :] = km_ref[...]
        scores = lax.dot_general(qbd, pad_ref[...], _NT, precision=HI, preferred_element_type=F32)
        sel_ref[...] = _topk_mask(scores, lane < nblk, lane, MOBA_TOPK)
        pad_ref[...] = jnp.zeros_like(pad_ref)
        pad_ref[0:nq, :] = kn_ref[...]
        logit = lax.dot_general(qbd.astype(BF16), pad_ref[...].astype(BF16), _NT, preferred_element_type=F32)
        t_row = lax.broadcasted_iota(jnp.int32, (rows, 128), 0) & (nq - 1)
        logit = jnp.where(lane <= t_row, logit, NEG)
        m0 = jnp.max(logit, axis=-1, keepdims=True)
        p0 = jnp.exp(logit - m0)
        m_ref[...] = m0
        l_ref[...] = jnp.sum(p0, axis=-1, keepdims=True)
        pad_ref[...] = jnp.zeros_like(pad_ref)
        pad_ref[0:nq, :] = vn_ref[...]
        acc_ref[...] = jnp.dot(p0.astype(BF16), pad_ref[...].astype(BF16), preferred_element_type=F32)

    gate = jnp.max(jnp.where(lane == n, sel_ref[...], 0.0), axis=-1, keepdims=True) > 0.5
    kblk = jnp.concatenate([ka_ref[...], kb_ref[...]], axis=0).astype(BF16)
    vblk = jnp.concatenate([va_ref[...], vb_ref[...]], axis=0).astype(BF16)
    lg = lax.dot_general(qbd_ref[...].astype(BF16), kblk, _NT, preferred_element_type=F32)
    lg = jnp.where(gate, lg, NEG)
    m = m_ref[...]
    m_new = jnp.maximum(m, jnp.max(lg, axis=-1, keepdims=True))
    alpha = jnp.exp(m - m_new)
    p = jnp.exp(lg - m_new)
    l_ref[...] = alpha * l_ref[...] + jnp.sum(p, axis=-1, keepdims=True)
    m_ref[...] = m_new
    acc_ref[...] = alpha * acc_ref[...] + jnp.dot(p.astype(BF16), vblk, preferred_element_type=F32)

    @pl.when(n == nblk - 1)
    def _():
        out = acc_ref[...] / l_ref[...]
        for h in range(MOBA_HEADS):
            kv = h // MOBA_GROUP
            o_ref[:, h * MOBA_HD:(h + 1) * MOBA_HD] = out[h * nq:(h + 1) * nq, kv * MOBA_HD:(kv + 1) * MOBA_HD]


def _moba_sample(qn, kn, proj, cache_k, cache_v, page_table, row0, nseq, nq):
    kvw = MOBA_KV * MOBA_HD
    n_pages = page_table.shape[1]
    nblk = n_pages * PAGE // MOBA_BLOCK
    ppb = MOBA_BLOCK // PAGE
    assert ppb == 2
    ck = cache_k.reshape(-1, PAGE, kvw)
    cv = cache_v.reshape(-1, PAGE, kvw)
    page_a = pl.BlockSpec((None, PAGE, kvw), lambda b, n, pt: (pt[b, 2 * n], 0, 0))
    page_b = pl.BlockSpec((None, PAGE, kvw), lambda b, n, pt: (pt[b, 2 * n + 1], 0, 0))
    kmean = pl.pallas_call(
        _kmean_kernel,
        grid_spec=pltpu.PrefetchScalarGridSpec(
            num_scalar_prefetch=1, grid=(nseq, nblk),
            in_specs=[page_a, page_b],
            out_specs=pl.BlockSpec((None, None, 1, kvw), lambda b, n, pt: (b, n, 0, 0))),
        out_shape=jax.ShapeDtypeStruct((nseq, nblk, 1, kvw), F32),
        compiler_params=_cp("parallel", "parallel"),
        name="moba_kmean",
    )(page_table, ck, ck).reshape(nseq, nblk, kvw)
    rb = row0 // nq
    rows = MOBA_HEADS * nq
    return pl.pallas_call(
        _moba_sample_kernel,
        grid_spec=pltpu.PrefetchScalarGridSpec(
            num_scalar_prefetch=1, grid=(nseq, nblk),
            in_specs=[pl.BlockSpec((nq, MOBA_HEADS * MOBA_HD), lambda b, n, pt: (rb + b, 0)),
                      pl.BlockSpec((nq, kvw), lambda b, n, pt: (rb + b, 0)),
                      pl.BlockSpec((nq, kvw), lambda b, n, pt: (rb + b, C_V // kvw)),
                      pl.BlockSpec((None, nblk, kvw), lambda b, n, pt: (b, 0, 0)),
                      page_a, page_b, page_a, page_b],
            out_specs=pl.BlockSpec((nq, MOBA_HEADS * MOBA_HD), lambda b, n, pt: (b, 0)),
            scratch_shapes=[pltpu.VMEM((rows, kvw), F32),
                            pltpu.VMEM((128, kvw), F32),
                            pltpu.VMEM((rows, 128), F32),
                            pltpu.VMEM((rows, 1), F32),
                            pltpu.VMEM((rows, 1), F32),
                            pltpu.VMEM((rows, kvw), F32)]),
        out_shape=jax.ShapeDtypeStruct((nseq * nq, MOBA_HEADS * MOBA_HD), F32),
        compiler_params=_cp("parallel", "arbitrary"),
        name="moba_sample",
    )(page_table, qn, kn, proj, kmean, ck, ck, cv, cv)


def _ssd_kernel(*refs, q, qin, nc, has_init):
    if has_init:
        (z_ref, x_ref, b_ref, c_ref, dt_ref, wx_ref, wb_ref, wc_ref, bx_ref, bb_ref, bc_ref, par_ref, gn_ref,
         s0_ref, cx_ref, cb0_ref, cc0_ref, y_ref, so_ref, ext_ref, st_ref, tr_ref) = refs
    else:
        (z_ref, x_ref, b_ref, c_ref, dt_ref, wx_ref, wb_ref, wc_ref, bx_ref, bb_ref, bc_ref, par_ref, gn_ref,
         y_ref, so_ref, ext_ref, st_ref, tr_ref) = refs
    g = pl.program_id(1)
    c = pl.program_id(2)
    gw = SSD_GW
    ns = SSD_STATE
    cw = gw + 2 * ns

    @pl.when(c == 0)
    def _():
        if has_init:
            st_ref[...] = s0_ref[...]
            ext_ref[0:8, 0:gw] = cx_ref[...]
            ext_ref[0:8, gw:gw + ns] = cb0_ref[...]
            ext_ref[0:8, gw + ns:cw] = cc0_ref[...]
        else:
            st_ref[...] = jnp.zeros_like(st_ref)
            ext_ref[0:8, :] = jnp.zeros((8, cw), F32)
        if qin < q:
            ext_ref[8 + qin:8 + q, :] = jnp.zeros((q - qin, cw), F32)

    ext_ref[8:8 + qin, 0:gw] = x_ref[...]
    ext_ref[8:8 + qin, gw:gw + ns] = b_ref[...]
    ext_ref[8:8 + qin, gw + ns:cw] = c_ref[...]
    w = jnp.concatenate([wx_ref[...], wb_ref[...], wc_ref[...]], axis=1)
    bias = jnp.concatenate([bx_ref[...], bb_ref[...], bc_ref[...]], axis=1)
    conv = bias + ext_ref[5:5 + q, :] * w[0:1]
    conv = conv + ext_ref[6:6 + q, :] * w[1:2]
    conv = conv + ext_ref[7:7 + q, :] * w[2:3]
    conv = conv + ext_ref[8:8 + q, :] * w[3:4]
    if nc > 1:
        ext_ref[0:8, :] = ext_ref[qin:qin + 8, :]
    xbc = conv * jax.nn.sigmoid(conv)
    xs = xbc[:, 0:gw]
    bm = xbc[:, gw:gw + ns]
    cm = xbc[:, gw + ns:cw]

    par = par_ref[...]
    dt = jax.nn.softplus(dt_ref[...] + par[0:1, :])
    if qin < q:
        dt = jnp.concatenate([dt, jnp.zeros((q - qin, 128), F32)], axis=0)
    dta = dt * (-jnp.exp(par[1:2, :]))
    ri = lax.broadcasted_iota(jnp.int32, (q, q), 0)
    ci = lax.broadcasted_iota(jnp.int32, (q, q), 1)
    causal = ri >= ci
    acs = jnp.dot(causal.astype(F32), dta, precision=HI, preferred_element_type=F32)
    acs_last = acs[q - 1:q, :]
    to_end = jnp.exp(acs_last - acs) * dt
    eacs = jnp.exp(acs)
    er = lax.broadcasted_iota(jnp.int32, (128, gw), 0)
    ec = lax.broadcasted_iota(jnp.int32, (128, gw), 1)
    expand = (er == g * 8 + (ec >> 6)).astype(F32)
    to_end_x = jnp.dot(to_end, expand, precision=HI, preferred_element_type=F32)
    eacs_x = jnp.dot(eacs, expand, precision=HI, preferred_element_type=F32)
    misc = jnp.concatenate([par[2:3, :], jnp.exp(acs_last), jnp.zeros((6, 128), F32)], axis=0)
    misc_x = jnp.dot(misc, expand, precision=HI, preferred_element_type=F32)
    tr_ref[0] = acs.T
    tr_ref[1] = dt.T
    g8 = pl.multiple_of(g * 8, 8)
    acs_t = tr_ref[0, pl.ds(g8, 8), :]
    dt_t = tr_ref[1, pl.ds(g8, 8), :]
    lane = lax.broadcasted_iota(jnp.int32, (q, 128), 1)
    low = lane < SSD_HD

    cb = lax.dot_general(cm.astype(BF16), bm.astype(BF16), _NT, preferred_element_type=F32)
    ys = []
    for pr in range(4):
        ms = []
        for r in (2 * pr, 2 * pr + 1):
            col = jnp.sum(jnp.where(lane == g * 8 + r, acs, 0.0), axis=1, keepdims=True)
            seg = col - acs_t[r:r + 1, :]
            dec = jnp.where(causal, jnp.exp(jnp.minimum(seg, 0.0)), 0.0)
            ms.append((cb * dec * dt_t[r:r + 1, :]).astype(BF16))
        lhs = jnp.concatenate(ms, axis=1)
        xp = xs[:, pr * 128:(pr + 1) * 128]
        rhs = jnp.concatenate([jnp.where(low, xp, 0.0), jnp.where(low, 0.0, xp)], axis=0).astype(BF16)
        ys.append(jnp.dot(lhs, rhs, preferred_element_type=F32))
    y = jnp.concatenate(ys, axis=1)
    st = st_ref[...]
    y = y + jnp.dot(cm.astype(BF16), st.astype(BF16), preferred_element_type=F32) * eacs_x
    y = y + misc_x[0:1, :] * xs
    if qin < q:
        y = y[0:qin, :]
    z = z_ref[...]
    y = y * (z * jax.nn.sigmoid(z))
    msq = jnp.mean(y * y, axis=-1, keepdims=True)
    y_ref[...] = ((y * lax.rsqrt(msq + EPS)) * gn_ref[...]).astype(y_ref.dtype)

    xw = (xs * to_end_x).astype(BF16)
    new = lax.dot_general(bm.astype(BF16), xw, _TN, preferred_element_type=F32)
    st_ref[...] = st * misc_x[1:2, :] + new

    @pl.when(c == nc - 1)
    def _():
        so_ref[...] = st_ref[...]


def _ssd(proj, row0, nb, nc, q, qin, par, conv_w, conv_b, gn, out_dtype, s0t=None, cst=None):
    gw, ns = SSD_GW, SSD_STATE
    rb = row0 // qin
    zb, xb = C_Z // gw, C_XBC // gw
    bb, cb = (C_XBC + D_INNER) // ns, (C_XBC + D_INNER + SSD_GROUPS * ns) // ns
    dtb = C_DT // 128
    row = lambda b, g, c: rb + b * nc + c
    in_specs = [
        pl.BlockSpec((qin, gw), lambda b, g, c: (row(b, g, c), zb + g)),
        pl.BlockSpec((qin, gw), lambda b, g, c: (row(b, g, c), xb + g)),
        pl.BlockSpec((qin, ns), lambda b, g, c: (row(b, g, c), bb + g)),
        pl.BlockSpec((qin, ns), lambda b, g, c: (row(b, g, c), cb + g)),
        pl.BlockSpec((qin, 128), lambda b, g, c: (row(b, g, c), dtb)),
        pl.BlockSpec((4, gw), lambda b, g, c: (0, g)),
        pl.BlockSpec((4, ns), lambda b, g, c: (0, D_INNER // ns + g)),
        pl.BlockSpec((4, ns), lambda b, g, c: (0, D_INNER // ns + SSD_GROUPS + g)),
        pl.BlockSpec((1, gw), lambda b, g, c: (0, g)),
        pl.BlockSpec((1, ns), lambda b, g, c: (0, D_INNER // ns + g)),
        pl.BlockSpec((1, ns), lambda b, g, c: (0, D_INNER // ns + SSD_GROUPS + g)),
        pl.BlockSpec((8, 128), lambda b, g, c: (0, 0)),
        pl.BlockSpec((1, gw), lambda b, g, c: (0, g)),
    ]
    args = [proj, proj, proj, proj, proj, conv_w, conv_w, conv_w, conv_b, conv_b, conv_b, par, gn]
    has_init = s0t is not None
    if has_init:
        in_specs += [
            pl.BlockSpec((None, None, ns, gw), lambda b, g, c: (b, g, 0, 0)),
            pl.BlockSpec((None, 8, gw), lambda b, g, c: (b, 0, g)),
            pl.BlockSpec((None, 8, ns), lambda b, g, c: (b, 0, D_INNER // ns + g)),
            pl.BlockSpec((None, 8, ns), lambda b, g, c: (b, 0, D_INNER // ns + SSD_GROUPS + g)),
        ]
        args += [s0t, cst, cst, cst]
    return pl.pallas_call(
        functools.partial(_ssd_kernel, q=q, qin=qin, nc=nc, has_init=has_init),
        grid=(nb, SSD_GROUPS, nc),
        in_specs=in_specs,
        out_specs=[pl.BlockSpec((qin, gw), lambda b, g, c: (b * nc + c, g)),
                   pl.BlockSpec((None, None, ns, gw), lambda b, g, c: (b, g, 0, 0))],
        out_shape=[jax.ShapeDtypeStruct((nb * nc * qin, D_INNER), out_dtype),
                   jax.ShapeDtypeStruct((nb, SSD_GROUPS, ns, gw), F32)],
        scratch_shapes=[pltpu.VMEM((q + 8, gw + 2 * ns), F32),
                        pltpu.VMEM((ns, gw), F32),
                        pltpu.VMEM((2, 128, q), F32)],
        compiler_params=_cp("parallel", "parallel", "arbitrary"),
        name="ssd",
    )(*args)


def _mem_attn_kernel(q_ref, k_ref, v_ref, o_ref):
    s = lax.dot_general(q_ref[...].astype(BF16), k_ref[...].astype(BF16), _NT, preferred_element_type=F32)
    m = jnp.max(s, axis=-1, keepdims=True)
    p = jnp.exp(s - m)
    l = jnp.sum(p, axis=-1, keepdims=True)
    o = jnp.dot(p.astype(BF16), v_ref[...].astype(BF16), preferred_element_type=F32) / l
    o_ref[...] = o.astype(o_ref.dtype)


def _mem_attn(qmn, row0, nb, seq, tq, k2d, k_cb, v2d, v_cb, out_dtype):
    nt = seq // tq
    rb = row0 // tq
    return pl.pallas_call(
        _mem_attn_kernel,
        grid=(nb, nt, MEM_HEADS),
        in_specs=[pl.BlockSpec((tq, MEM_HD), lambda b, i, h: (rb + b * nt + i, h)),
                  pl.BlockSpec((MEM_TOKENS, MEM_HD), lambda b, i, h: (b, k_cb + h)),
                  pl.BlockSpec((MEM_TOKENS, MEM_HD), lambda b, i, h: (b, v_cb + h))],
        out_specs=pl.BlockSpec((tq, MEM_HD), lambda b, i, h: (b * nt + i, h)),
        out_shape=jax.ShapeDtypeStruct((nb * seq, MEM_HEADS * MEM_HD), out_dtype),
        compiler_params=_cp("parallel", "parallel", "parallel"),
        name="mem_attn",
    )(qmn, k2d, v2d)


def _merge_kernel(a_ref, y_ref, m_ref, wa_ref, wb_ref, wc_ref, g0_ref, g1_ref, g2_ref,
                  b0_ref, b1_ref, b2_ref, o_ref):
    oa = jnp.dot(a_ref[...], wa_ref[...], preferred_element_type=F32)
    ob = jnp.dot(y_ref[...], wb_ref[...], preferred_element_type=F32)
    oc = jnp.dot(m_ref[...], wc_ref[...], preferred_element_type=F32)
    o = jax.nn.sigmoid(g0_ref[...] + b0_ref[...]) * oa
    o = o + jax.nn.sigmoid(g1_ref[...] + b1_ref[...]) * ob
    o = o + jax.nn.sigmoid(g2_ref[...] + b2_ref[...]) * oc
    o_ref[...] = o.astype(o_ref.dtype)


def _merge(attn, yssd, memo, wa, wb, wc, proj, bgate, tm=528, tn=512):
    m = attn.shape[0]
    nj = D_MODEL // tn
    gcb = C_GATE // tn
    act = lambda k: pl.BlockSpec((tm, k), lambda i, j: (i, 0))
    wsp = lambda k: pl.BlockSpec((k, tn), lambda i, j: (0, j))
    gsp = lambda br: pl.BlockSpec((tm, tn), lambda i, j: (i, gcb + br * nj + j))
    bsp = lambda br: pl.BlockSpec((1, tn), lambda i, j: (0, br * nj + j))
    return pl.pallas_call(
        _merge_kernel,
        grid=(m // tm, nj),
        in_specs=[act(attn.shape[1]), act(yssd.shape[1]), act(memo.shape[1]),
                  wsp(wa.shape[0]), wsp(wb.shape[0]), wsp(wc.shape[0]),
                  gsp(0), gsp(1), gsp(2), bsp(0), bsp(1), bsp(2)],
        out_specs=pl.BlockSpec((tm, tn), lambda i, j: (i, j)),
        out_shape=jax.ShapeDtypeStruct((m, D_MODEL), BF16),
        compiler_params=_cp("parallel", "parallel"),
        name="merge",
    )(attn, yssd, memo, wa, wb, wc, proj, proj, proj, bgate, bgate, bgate)


def _router_kernel(x_ref, g_ref, w_ref, b_ref, h_ref, idx_ref, p_ref):
    x = x_ref[...]
    ms = jnp.mean(x * x, axis=-1, keepdims=True)
    h = (x * lax.rsqrt(ms + EPS)) * g_ref[...]
    h_ref[...] = h
    logits = jnp.dot(h, w_ref[...], precision=HI, preferred_element_type=F32) + b_ref[...]
    lane = lax.broadcasted_iota(jnp.int32, logits.shape, 1)
    lanef = lane.astype(F32)
    s = jnp.where(lane < N_EXPERTS, logits, -jnp.inf)
    vals, idxs = [], []
    for _ in range(TOP_K):
        m = jnp.max(s, axis=-1, keepdims=True)
        idx = jnp.min(jnp.where(s == m, lanef, 128.0), axis=-1, keepdims=True)
        vals.append(m)
        idxs.append(idx)
        s = jnp.where(lanef == idx, -jnp.inf, s)
    es = [jnp.exp(v - vals[0]) for v in vals]
    den = es[0] + es[1] + es[2] + es[3]
    p_out = jnp.zeros(logits.shape, F32)
    i_out = jnp.zeros(logits.shape, F32)
    for k in range(TOP_K):
        p_out = jnp.where(lane == k, es[k] / den, p_out)
        i_out = jnp.where(lane == k, idxs[k], i_out)
    p_ref[...] = p_out
    idx_ref[...] = i_out.astype(jnp.int32)


def _router(x2, g, w_router, b_router, tr=264):
    m, d = x2.shape
    wr = jnp.zeros((d, 128), F32).at[:, :N_EXPERTS].set(w_router)
    br = jnp.zeros((1, 128), F32).at[0, :N_EXPERTS].set(b_router)
    return pl.pallas_call(
        _router_kernel,
        grid=(m // tr,),
        in_specs=[pl.BlockSpec((tr, d), lambda i: (i, 0)),
                  pl.BlockSpec((1, d), lambda i: (0, 0)),
                  pl.BlockSpec((d, 128), lambda i: (0, 0)),
                  pl.BlockSpec((1, 128), lambda i: (0, 0))],
        out_specs=[pl.BlockSpec((tr, d), lambda i: (i, 0)),
                   pl.BlockSpec((tr, 128), lambda i: (i, 0)),
                   pl.BlockSpec((tr, 128), lambda i: (i, 0))],
        out_shape=[jax.ShapeDtypeStruct((m, d), F32),
                   jax.ShapeDtypeStruct((m, 128), jnp.int32),
                   jax.ShapeDtypeStruct((m, 128), F32)],
        compiler_params=_cp("parallel"),
        name="router",
    )(x2, g.reshape(1, d), wr, br)


MOE_TM = 256
MOE_TN = 256
GATHER_ROWS = 256
COMBINE_TOKENS = 64


def _dispatch(ridx, rprob, tm, n_tiles, nj):
    t = ridx.shape[0]
    na = t * TOP_K
    e_flat = ridx.reshape(na)
    counts = jnp.zeros((N_EXPERTS,), jnp.int32).at[e_flat].add(1)
    nt = (counts + tm - 1) // tm
    t_end = jnp.cumsum(nt)
    t0 = t_end - nt
    n_used = t_end[-1]
    order = jnp.argsort(e_flat, stable=True)
    e_sorted = e_flat[order]
    start = jnp.cumsum(counts) - counts
    dest_sorted = t0[e_sorted] * tm + (jnp.arange(na, dtype=jnp.int32) - start[e_sorted])
    dest = jnp.zeros((na,), jnp.int32).at[order].set(dest_sorted)
    tok = jnp.zeros((n_tiles * tm,), jnp.int32).at[dest].set(jnp.arange(na, dtype=jnp.int32) // TOP_K)
    prob = jnp.zeros((n_tiles * tm,), F32).at[dest].set(rprob.reshape(na))
    tile_e = jnp.minimum(jnp.searchsorted(t_end, jnp.arange(n_tiles, dtype=jnp.int32), side="right"),
                         N_EXPERTS - 1).astype(jnp.int32)
    s = jnp.arange(n_tiles * nj, dtype=jnp.int32)
    valid = (s < nj * n_used).astype(jnp.int32)
    sc = jnp.minimum(s, nj * n_used - 1)
    e = tile_e[sc // nj]
    local = sc - nj * t0[e]
    sj = local // nt[e]
    si = t0[e] + local % nt[e]
    first = ((local % nt[e]) == 0).astype(jnp.int32)
    so_i = jnp.where(valid == 1, si, s // nj)
    so_j = jnp.where(valid == 1, sj, s % nj)
    return tok, prob.reshape(-1, 1), dest, (e, sj, si, first, valid, so_i, so_j)


def _gather_kernel(tok_ref, h_hbm, o_ref, buf_ref, sem):
    base = pl.program_id(0) * GATHER_ROWS

    def row_copy(r, src_row):
        return pltpu.make_async_copy(h_hbm.at[pl.ds(src_row, 1), :], buf_ref.at[pl.ds(r, 1), :], sem)

    def start(r, carry):
        row_copy(r, tok_ref[base + r]).start()
        return carry

    def wait(r, carry):
        row_copy(r, 0).wait()
        return carry

    lax.fori_loop(0, GATHER_ROWS, start, 0)
    lax.fori_loop(0, GATHER_ROWS, wait, 0)
    o_ref[...] = buf_ref[...].astype(o_ref.dtype)


def _gather_rows(tok, h2):
    rows = tok.shape[0]
    d = h2.shape[1]
    return pl.pallas_call(
        _gather_kernel,
        grid_spec=pltpu.PrefetchScalarGridSpec(
            num_scalar_prefetch=1, grid=(rows // GATHER_ROWS,),
            in_specs=[pl.BlockSpec(memory_space=pl.ANY)],
            out_specs=pl.BlockSpec((GATHER_ROWS, d), lambda i, tok: (i, 0)),
            scratch_shapes=[pltpu.VMEM((GATHER_ROWS, d), F32), pltpu.SemaphoreType.DMA(())]),
        out_shape=jax.ShapeDtypeStruct((rows, d), BF16),
        compiler_params=_cp("arbitrary"),
        name="moe_gather",
    )(tok, h2)


def _moe_up_kernel(se, sj, si, sfirst, svalid, so_i, so_j, x_ref, wg_ref, wu_ref, bg_ref, bu_ref, o_ref, wg16, wu16):
    s = pl.program_id(0)

    @pl.when(svalid[s] == 1)
    def _():
        @pl.when(sfirst[s] == 1)
        def _():
            wg16[...] = wg_ref[...].astype(BF16)
            wu16[...] = wu_ref[...].astype(BF16)

        x = x_ref[...]
        g = jnp.minimum(jnp.dot(x, wg16[...], preferred_element_type=F32) + bg_ref[...], SWIGLU_LIMIT)
        u = jnp.clip(jnp.dot(x, wu16[...], preferred_element_type=F32) + bu_ref[...], -SWIGLU_LIMIT, SWIGLU_LIMIT)
        act = (u + 1.0) * (g * jax.nn.sigmoid(SWIGLU_ALPHA * g))
        o_ref[...] = act.astype(o_ref.dtype)

    @pl.when(svalid[s] == 0)
    def _():
        o_ref[...] = jnp.zeros_like(o_ref)


def _moe_down_kernel(se, sj, si, sfirst, svalid, so_i, so_j, a_ref, w_ref, b_ref, p_ref, o_ref, w16):
    s = pl.program_id(0)

    @pl.when(svalid[s] == 1)
    def _():
        @pl.when(sfirst[s] == 1)
        def _():
            w16[...] = w_ref[...].astype(BF16)

        y = jnp.dot(a_ref[...], w16[...], preferred_element_type=F32) + b_ref[...]
        o_ref[...] = p_ref[...] * y

    @pl.when(svalid[s] == 0)
    def _():
        o_ref[...] = jnp.zeros_like(o_ref)


def _moe_up(sched, xs, wg, wu, bg, bu, tm, tn):
    rows, d = xs.shape
    f = wg.shape[2]
    nsteps = sched[0].shape[0]
    xsp = pl.BlockSpec((tm, d), lambda s, se, sj, si, sf, sv, oi, oj: (si[s], 0))
    wsp = pl.BlockSpec((None, d, tn), lambda s, se, sj, si, sf, sv, oi, oj: (se[s], 0, sj[s]))
    bsp = pl.BlockSpec((None, 1, tn), lambda s, se, sj, si, sf, sv, oi, oj: (se[s], 0, sj[s]))
    return pl.pallas_call(
        _moe_up_kernel,
        grid_spec=pltpu.PrefetchScalarGridSpec(
            num_scalar_prefetch=7, grid=(nsteps,),
            in_specs=[xsp, wsp, wsp, bsp, bsp],
            out_specs=pl.BlockSpec((tm, tn), lambda s, se, sj, si, sf, sv, oi, oj: (oi[s], oj[s])),
            scratch_shapes=[pltpu.VMEM((d, tn), BF16), pltpu.VMEM((d, tn), BF16)]),
        out_shape=jax.ShapeDtypeStruct((rows, f), BF16),
        compiler_params=_cp("arbitrary"),
        name="moe_up",
    )(*sched, xs, wg, wu, bg, bu)


def _moe_down(sched, act, wd, bd, prob, tm, tn):
    rows, f = act.shape
    d = wd.shape[2]
    nsteps = sched[0].shape[0]
    return pl.pallas_call(
        _moe_down_kernel,
        grid_spec=pltpu.PrefetchScalarGridSpec(
            num_scalar_prefetch=7, grid=(nsteps,),
            in_specs=[pl.BlockSpec((tm, f), lambda s, se, sj, si, sf, sv, oi, oj: (si[s], 0)),
                      pl.BlockSpec((None, f, tn), lambda s, se, sj, si, sf, sv, oi, oj: (se[s], 0, sj[s])),
                      pl.BlockSpec((None, 1, tn), lambda s, se, sj, si, sf, sv, oi, oj: (se[s], 0, sj[s])),
                      pl.BlockSpec((tm, 1), lambda s, se, sj, si, sf, sv, oi, oj: (si[s], 0))],
            out_specs=pl.BlockSpec((tm, tn), lambda s, se, sj, si, sf, sv, oi, oj: (oi[s], oj[s])),
            scratch_shapes=[pltpu.VMEM((f, tn), BF16)]),
        out_shape=jax.ShapeDtypeStruct((rows, d), F32),
        compiler_params=_cp("arbitrary"),
        name="moe_down",
    )(*sched, act, wd, bd, prob)


def _combine_kernel(pos_ref, x_ref, y_hbm, o_ref, buf_ref, sem):
    n = COMBINE_TOKENS * TOP_K
    base = pl.program_id(0) * n

    def row_copy(r, src_row):
        return pltpu.make_async_copy(y_hbm.at[pl.ds(src_row, 1), :],
                                     buf_ref.at[r & (TOP_K - 1), pl.ds(r >> 2, 1), :], sem)

    def start(r, carry):
        row_copy(r, pos_ref[base + r]).start()
        return carry

    def wait(r, carry):
        row_copy(r, 0).wait()
        return carry

    lax.fori_loop(0, n, start, 0)
    lax.fori_loop(0, n, wait, 0)
    o_ref[...] = x_ref[...] + ((buf_ref[0] + buf_ref[1]) + (buf_ref[2] + buf_ref[3]))


def _combine(pos, x2, ysort):
    t, d = x2.shape
    assert TOP_K == 4
    return pl.pallas_call(
        _combine_kernel,
        grid_spec=pltpu.PrefetchScalarGridSpec(
            num_scalar_prefetch=1, grid=(t // COMBINE_TOKENS,),
            in_specs=[pl.BlockSpec((COMBINE_TOKENS, d), lambda i, pos: (i, 0)),
                      pl.BlockSpec(memory_space=pl.ANY)],
            out_specs=pl.BlockSpec((COMBINE_TOKENS, d), lambda i, pos: (i, 0)),
            scratch_shapes=[pltpu.VMEM((TOP_K, COMBINE_TOKENS, d), F32), pltpu.SemaphoreType.DMA(())]),
        out_shape=jax.ShapeDtypeStruct((t, d), F32),
        compiler_params=_cp("arbitrary"),
        name="moe_combine",
    )(pos, x2, ysort)


def _moe(x2, norm_g, w_router, b_router, wg, bg, wu, bu, wd, bd):
    t = x2.shape[0]
    h2, ridx, rprob = _router(x2, norm_g, w_router, b_router)
    n_tiles = (t * TOP_K) // MOE_TM + N_EXPERTS
    nj = D_FF // MOE_TN
    tok, prob, pos, sched = _dispatch(ridx[:, :TOP_K], rprob[:, :TOP_K], MOE_TM, n_tiles, nj)
    xs = _gather_rows(tok, h2)
    act = _moe_up(sched, xs, wg, wu, bg.reshape(N_EXPERTS, 1, D_FF), bu.reshape(N_EXPERTS, 1, D_FF),
                  MOE_TM, MOE_TN)
    ysort = _moe_down(sched, act, wd, bd.reshape(N_EXPERTS, 1, D_MODEL), prob, MOE_TM, MOE_TN)
    return _combine(pos, x2, ysort)


def _state_to_kernel_layout(s):
    n = s.shape[0]
    s = s.reshape(n, SSD_GROUPS, SSD_HEADS // SSD_GROUPS, SSD_HD, SSD_STATE)
    return s.transpose(0, 1, 4, 2, 3).reshape(n, SSD_GROUPS, SSD_STATE, SSD_GW)


def _state_from_kernel_layout(s):
    n = s.shape[0]
    s = s.reshape(n, SSD_GROUPS, SSD_STATE, SSD_HEADS // SSD_GROUPS, SSD_HD)
    return s.transpose(0, 1, 3, 4, 2).reshape(n, SSD_HEADS, SSD_HD, SSD_STATE)


def kernel(x_prompt, x_sample, mem_prompt, cache_k, cache_v, page_table, state_ssm, state_conv, cache_mem_k, cache_mem_v, norm_mix_g, w_in, b_branch_gate, moba_q_norm_g, moba_k_norm_g, conv_w, conv_b, dt_bias, a_log, d_skip, ssd_norm_g, mem_norm_g, w_mem_kv, mem_q_norm_g, mem_k_norm_g, w_br_moba, w_br_ssd, w_br_mem, w_out, norm_ffn_g, w_router, b_router, w_exp_gate, b_exp_gate, w_exp_up, b_exp_up, w_exp_down, b_exp_down):
    assert w_in.shape[0] == 1, "single layer"
    nb, seq, d = x_prompt.shape
    ns, nq, _ = x_sample.shape
    tp = nb * seq
    x_all = jnp.concatenate([x_prompt.reshape(tp, d), x_sample.reshape(ns * nq, d)], axis=0)
    t = x_all.shape[0]

    w_in2 = w_in.reshape(d, -1)
    w_in_p = jnp.concatenate([w_in2[:, :W_IN_SPLIT].astype(BF16),
                              jnp.zeros((d, C_QM - W_IN_SPLIT), BF16),
                              w_in2[:, W_IN_SPLIT:].astype(BF16)], axis=1)
    h = _rmsnorm(x_all, norm_mix_g[0], BF16)
    proj = _matmul(h, w_in_p, tm=1056, tn=512, name="in_proj")

    qn = _head_norm(proj, C_Q, MOBA_HEADS * MOBA_HD, MOBA_HD, moba_q_norm_g[0], MOBA_HD ** -0.5)
    kn = _head_norm(proj, C_K, MOBA_KV * MOBA_HD, MOBA_HD, moba_k_norm_g[0], 1.0)
    qmn = _head_norm(proj, C_QM, MEM_HEADS * MEM_HD, MEM_HD, mem_q_norm_g[0], MEM_HD ** -0.5)

    attn_p = _moba_prompt(qn, kn, proj, nb, seq)
    attn_s = _moba_sample(qn, kn, proj, cache_k, cache_v, page_table, tp, ns, nq)
    attn = jnp.concatenate([attn_p, attn_s.astype(BF16)], axis=0)

    par = jnp.zeros((8, 128), F32)
    par = par.at[0, :SSD_HEADS].set(dt_bias[0]).at[1, :SSD_HEADS].set(a_log[0]).at[2, :SSD_HEADS].set(d_skip[0])
    gn = ssd_norm_g.reshape(1, D_INNER)
    cw2 = conv_w.reshape(-1, CONV_CH)
    cb2 = conv_b.reshape(1, CONV_CH)
    y_p, st_p = _ssd(proj, 0, nb, seq // 256, 256, 256, par, cw2, cb2, gn, BF16)
    s0t = _state_to_kernel_layout(state_ssm.reshape((ns,) + state_ssm.shape[2:]))
    cst = jnp.concatenate([jnp.zeros((ns, 5, CONV_CH), F32), state_conv.reshape(ns, 3, CONV_CH)], axis=1)
    y_s, st_s = _ssd(proj, tp, ns, 1, 128, nq, par, cw2, cb2, gn, F32, s0t=s0t, cst=cst)
    yssd = jnp.concatenate([y_p, y_s.astype(BF16)], axis=0)

    hm = _rmsnorm(mem_prompt.reshape(nb * MEM_TOKENS, d), mem_norm_g[0], BF16)
    memkv = _matmul(hm, w_mem_kv.reshape(d, -1).astype(BF16), tm=512, tn=512, name="mem_kv")
    mw = MEM_HEADS * MEM_HD
    mk = _head_norm(memkv, 0, mw, MEM_HD, mem_k_norm_g[0], 1.0, tr=256)
    memo_p = _mem_attn(qmn, 0, nb, seq, 256, mk, 0, memkv, MEM_HEADS, BF16)
    cmk = cache_mem_k.reshape(ns * MEM_TOKENS, mw)
    cmv = cache_mem_v.reshape(ns * MEM_TOKENS, mw)
    memo_s = _mem_attn(qmn, tp, ns, nq, nq, cmk, 0, cmv, 0, F32)
    memo = jnp.concatenate([memo_p, memo_s.astype(BF16)], axis=0)

    merged = _merge(attn, yssd, memo, w_br_moba.reshape(-1, d).astype(BF16), w_br_ssd.reshape(-1, d).astype(BF16),
                    w_br_mem.reshape(-1, d).astype(BF16), proj, b_branch_gate.reshape(1, -1))
    x2 = _matmul(merged, w_out.reshape(d, d).astype(BF16), tm=1056, tn=512, residual=x_all, name="out_proj")

    y = _moe(x2, norm_ffn_g[0], w_router.reshape(d, N_EXPERTS), b_router[0],
             w_exp_gate.reshape(N_EXPERTS, d, D_FF), b_exp_gate.reshape(N_EXPERTS, D_FF),
             w_exp_up.reshape(N_EXPERTS, d, D_FF), b_exp_up.reshape(N_EXPERTS, D_FF),
             w_exp_down.reshape(N_EXPERTS, D_FF, d), b_exp_down.reshape(N_EXPERTS, d))

    kvw = MOBA_KV * MOBA_HD
    v_all = proj[:, C_V:C_V + kvw]
    xbc_raw = proj[:, C_XBC:C_XBC + CONV_CH]
    return (
        y[:tp].reshape(nb, seq, d),
        y[tp:].reshape(ns, nq, d),
        kn[:tp].reshape(1, nb, seq, MOBA_KV, MOBA_HD),
        v_all[:tp].reshape(1, nb, seq, MOBA_KV, MOBA_HD),
        kn[tp:].reshape(1, ns, nq, MOBA_KV, MOBA_HD),
        v_all[tp:].reshape(1, ns, nq, MOBA_KV, MOBA_HD),
        _state_from_kernel_layout(st_p)[None],
        xbc_raw[:tp].reshape(nb, seq, CONV_CH)[:, seq - 3:][None],
        _state_from_kernel_layout(st_s)[None],
        xbc_raw[tp:].reshape(ns, nq, CONV_CH)[:, nq - 3:][None],
        mk.reshape(1, nb, MEM_TOKENS, MEM_HEADS, MEM_HD),
        memkv[:, mw:].reshape(1, nb, MEM_TOKENS, MEM_HEADS, MEM_HD),
    )
```

```python
import functools

import jax
import jax.numpy as jnp
from jax import lax
from jax.experimental import pallas as pl
from jax.experimental.pallas import tpu as pltpu

F32 = jnp.float32
BF16 = jnp.bfloat16
HI = lax.Precision.HIGHEST

D_MODEL = 4096
EPS = 1e-6
NEG = -1e30

MOBA_HEADS = 16
MOBA_KV = 4
MOBA_GROUP = 4
MOBA_HD = 128
MOBA_BLOCK = 256
MOBA_TOPK = 3
PAGE = 128
D_INNER = 4096
SSD_HEADS = 64
SSD_HD = 64
SSD_GROUPS = 8
SSD_STATE = 128
SSD_GW = D_INNER // SSD_GROUPS
CONV_CH = D_INNER + 2 * SSD_GROUPS * SSD_STATE
MEM_TOKENS = 256
MEM_HEADS = 4
MEM_HD = 512
N_EXPERTS = 32
TOP_K = 4
D_FF = 4096
SWIGLU_LIMIT = 7.0
SWIGLU_ALPHA = 1.702

C_Q = 0
C_K = 2048
C_V = 2560
C_Z = 3072
C_XBC = 7168
W_IN_A = 13312
W_IN_B = 13376
CB_QM = 0
CB_GATE = 2048

V7X_VMEM_LIMIT = 56 * 1024 * 1024

_NT = (((1,), (1,)), ((), ()))
_TN = (((0,), (0,)), ((), ()))


def _cp(*sem):
    return pltpu.CompilerParams(dimension_semantics=sem, vmem_limit_bytes=V7X_VMEM_LIMIT)


def _rmsnorm_kernel(x_ref, g_ref, o_ref):
    x = x_ref[...]
    ms = jnp.mean(x * x, axis=-1, keepdims=True)
    o_ref[...] = ((x * lax.rsqrt(ms + EPS)) * g_ref[...]).astype(o_ref.dtype)


def _rmsnorm(x, g, out_dtype, tr=256):
    m, d = x.shape
    return pl.pallas_call(
        _rmsnorm_kernel,
        grid=(m // tr,),
        in_specs=[pl.BlockSpec((tr, d), lambda i: (i, 0)), pl.BlockSpec((1, d), lambda i: (0, 0))],
        out_specs=pl.BlockSpec((tr, d), lambda i: (i, 0)),
        out_shape=jax.ShapeDtypeStruct((m, d), out_dtype),
        compiler_params=_cp("parallel"),
        name="rmsnorm",
    )(x, g.reshape(1, d))


def _head_norm_kernel(x_ref, g_ref, o_ref, *, hd, scale):
    x = x_ref[...]
    g = g_ref[...]
    for s in range(x.shape[1] // hd):
        xs = x[:, s * hd:(s + 1) * hd]
        ms = jnp.mean(xs * xs, axis=-1, keepdims=True)
        y = (xs * lax.rsqrt(ms + EPS)) * g[:, s * hd:(s + 1) * hd]
        if scale != 1.0:
            y = y * scale
        o_ref[:, s * hd:(s + 1) * hd] = y


def _head_norm(src, col_off, width, hd, gain, scale, tr=264):
    rows = src.shape[0]
    cb = col_off // 512
    return pl.pallas_call(
        functools.partial(_head_norm_kernel, hd=hd, scale=scale),
        grid=(rows // tr, width // 512),
        in_specs=[pl.BlockSpec((tr, 512), lambda i, j: (i, cb + j)),
                  pl.BlockSpec((1, 512), lambda i, j: (0, 0))],
        out_specs=pl.BlockSpec((tr, 512), lambda i, j: (i, j)),
        out_shape=jax.ShapeDtypeStruct((rows, width), F32),
        compiler_params=_cp("parallel", "parallel"),
        name="head_norm",
    )(src, jnp.tile(gain, 512 // hd).reshape(1, 512))


def _mm_kernel(a_ref, w_ref, o_ref):
    o_ref[...] = jnp.dot(a_ref[...], w_ref[...], preferred_element_type=F32).astype(o_ref.dtype)


def _mm_res_kernel(a_ref, w_ref, r_ref, o_ref):
    o_ref[...] = r_ref[...] + jnp.dot(a_ref[...], w_ref[...], preferred_element_type=F32)


def _matmul(a, w, tm, tn, out_dtype=F32, residual=None, name="matmul"):
    m, k = a.shape
    n = w.shape[1]
    in_specs = [pl.BlockSpec((tm, k), lambda i, j: (i, 0)), pl.BlockSpec((k, tn), lambda i, j: (0, j))]
    args = [a, w]
    body = _mm_kernel
    if residual is not None:
        in_specs.append(pl.BlockSpec((tm, tn), lambda i, j: (i, j)))
        args.append(residual)
        body = _mm_res_kernel
    return pl.pallas_call(
        body,
        grid=(m // tm, n // tn),
        in_specs=in_specs,
        out_specs=pl.BlockSpec((tm, tn), lambda i, j: (i, j)),
        out_shape=jax.ShapeDtypeStruct((m, n), out_dtype),
        compiler_params=_cp("parallel", "parallel"),
        name=name,
    )(*args)


def _mm_wcast_kernel(a_ref, w_ref, o_ref, w16_ref):
    @pl.when(pl.program_id(1) == 0)
    def _():
        w16_ref[...] = w_ref[...].astype(BF16)

    o_ref[...] = jnp.dot(a_ref[...], w16_ref[...], preferred_element_type=F32)


def _matmul_f32w(a, w, n, tm, tn, name):
    m, k = a.shape
    return pl.pallas_call(
        _mm_wcast_kernel,
        grid=(n // tn, m // tm),
        in_specs=[pl.BlockSpec((tm, k), lambda j, i: (i, 0)), pl.BlockSpec((k, tn), lambda j, i: (0, j))],
        out_specs=pl.BlockSpec((tm, tn), lambda j, i: (i, j)),
        out_shape=jax.ShapeDtypeStruct((m, n), F32),
        scratch_shapes=[pltpu.VMEM((k, tn), BF16)],
        compiler_params=_cp("parallel", "arbitrary"),
        name=name,
    )(a, w)


def _topk_mask(s, valid, lane, k):
    lanef = lane.astype(F32)
    s = jnp.where(valid, s, -jnp.inf)
    sel = jnp.zeros(s.shape, F32)
    for _ in range(k):
        m = jnp.max(s, axis=-1, keepdims=True)
        idx = jnp.min(jnp.where(s == m, lanef, 128.0), axis=-1, keepdims=True)
        hit = lanef == idx
        sel = jnp.where(hit, 1.0, sel)
        s = jnp.where(hit, -jnp.inf, s)
    return jnp.where(valid, sel, 0.0)


def _moba_prompt_kernel(q_ref, k_ref, v_ref, o_ref, kmean_ref, acc_ref):
    qb = pl.program_id(2)
    blk = MOBA_BLOCK
    grp = MOBA_GROUP
    rows = grp * blk
    nblk = k_ref.shape[0] // blk

    @pl.when(qb == 0)
    def _():
        kmean_ref[...] = jnp.zeros_like(kmean_ref)
        for n in range(nblk):
            kmean_ref[n:n + 1, :] = jnp.sum(k_ref[n * blk:(n + 1) * blk, :], axis=0, keepdims=True) * (1.0 / blk)

    q4 = q_ref[...]
    q = jnp.concatenate([q4[:, g * MOBA_HD:(g + 1) * MOBA_HD] for g in range(grp)], axis=0)
    scores = lax.dot_general(q, kmean_ref[...], _NT, precision=HI, preferred_element_type=F32)
    lane = lax.broadcasted_iota(jnp.int32, (rows, 128), 1)
    sel = _topk_mask(scores, lane < qb, lane, MOBA_TOPK)
    q16 = q.astype(BF16)

    own = pl.multiple_of(qb * blk, blk)
    k_own = k_ref[pl.ds(own, blk), :].astype(BF16)
    v_own = v_ref[pl.ds(own, blk), :].astype(BF16)
    logit = lax.dot_general(q16, k_own, _NT, preferred_element_type=F32)
    t_row = lax.broadcasted_iota(jnp.int32, (rows, blk), 0) & (blk - 1)
    j_col = lax.broadcasted_iota(jnp.int32, (rows, blk), 1)
    logit = jnp.where(j_col <= t_row, logit, NEG)
    m0 = jnp.max(logit, axis=-1, keepdims=True)
    p0 = jnp.exp(logit - m0)
    l0 = jnp.sum(p0, axis=-1, keepdims=True)
    acc_ref[...] = jnp.dot(p0.astype(BF16), v_own, preferred_element_type=F32)

    def body(n, carry):
        m, l = carry
        gate = jnp.max(jnp.where(lane == n, sel, 0.0), axis=-1, keepdims=True) > 0.5
        off = pl.multiple_of(n * blk, blk)
        kb = k_ref[pl.ds(off, blk), :].astype(BF16)
        vb = v_ref[pl.ds(off, blk), :].astype(BF16)
        lg = lax.dot_general(q16, kb, _NT, preferred_element_type=F32)
        lg = jnp.where(gate, lg, NEG)
        m_new = jnp.maximum(m, jnp.max(lg, axis=-1, keepdims=True))
        alpha = jnp.exp(m - m_new)
        p = jnp.exp(lg - m_new)
        l = alpha * l + jnp.sum(p, axis=-1, keepdims=True)
        acc_ref[...] = alpha * acc_ref[...] + jnp.dot(p.astype(BF16), vb, preferred_element_type=F32)
        return m_new, l

    _, l = lax.fori_loop(0, qb, body, (m0, l0))
    out = acc_ref[...] / l
    for g in range(grp):
        o_ref[:, g * MOBA_HD:(g + 1) * MOBA_HD] = out[g * blk:(g + 1) * blk, :].astype(o_ref.dtype)


def _moba_prompt(qn, kn, proj, nb, seq):
    nqb = seq // MOBA_BLOCK
    gw = MOBA_GROUP * MOBA_HD
    vcb = C_V // MOBA_HD
    return pl.pallas_call(
        _moba_prompt_kernel,
        grid=(nb, MOBA_KV, nqb),
        in_specs=[pl.BlockSpec((MOBA_BLOCK, gw), lambda b, h, i: (b * nqb + i, h)),
                  pl.BlockSpec((seq, MOBA_HD), lambda b, h, i: (b, h)),
                  pl.BlockSpec((seq, MOBA_HD), lambda b, h, i: (b, vcb + h))],
        out_specs=pl.BlockSpec((MOBA_BLOCK, gw), lambda b, h, i: (b * nqb + i, h)),
        out_shape=jax.ShapeDtypeStruct((nb * seq, MOBA_HEADS * MOBA_HD), BF16),
        scratch_shapes=[pltpu.VMEM((128, MOBA_HD), F32),
                        pltpu.VMEM((MOBA_GROUP * MOBA_BLOCK, MOBA_HD), F32)],
        compiler_params=_cp("parallel", "parallel", "arbitrary"),
        name="moba_prompt",
    )(qn, kn, proj)


SAMPLE_BLOCKS_PER_STEP = 4
_PAGES_PER_STEP = SAMPLE_BLOCKS_PER_STEP * (MOBA_BLOCK // PAGE)


def _kmean_kernel(pt_ref, *refs):
    pages = refs[:_PAGES_PER_STEP]
    o_ref = refs[_PAGES_PER_STEP]
    prow = PAGE * MOBA_KV
    r = lax.broadcasted_iota(jnp.int32, (8, prow), 1)
    h = lax.broadcasted_iota(jnp.int32, (8, prow), 0)
    pick = ((r & (MOBA_KV - 1)) == h).astype(F32)
    for i in range(SAMPLE_BLOCKS_PER_STEP):
        s = jnp.dot(pick, pages[2 * i][...], precision=HI, preferred_element_type=F32)
        s = s + jnp.dot(pick, pages[2 * i + 1][...], precision=HI, preferred_element_type=F32)
        o_ref[MOBA_KV * i:MOBA_KV * (i + 1), :] = s[0:MOBA_KV, :] * (1.0 / MOBA_BLOCK)


def _moba_sample_kernel(pt_ref, q_ref, kn_ref, vn_ref, km_ref, *refs):
    kp = refs[:_PAGES_PER_STEP]
    vp = refs[_PAGES_PER_STEP:2 * _PAGES_PER_STEP]
    o_ref, qa_ref, pad_ref, sel_ref, m_ref, l_ref, acc_ref = refs[2 * _PAGES_PER_STEP:]
    n = pl.program_id(1)
    nsteps = pl.num_programs(1)
    nq = q_ref.shape[0]
    rows = MOBA_HEADS * nq
    prow = PAGE * MOBA_KV
    lane = lax.broadcasted_iota(jnp.int32, (rows, 128), 1)
    rowi = lax.broadcasted_iota(jnp.int32, (rows, 128), 0)
    kv_shift = (MOBA_GROUP * nq).bit_length() - 1
    nq_shift = nq.bit_length() - 1
    row_kv = rowi >> kv_shift

    @pl.when(n == 0)
    def _():
        for h in range(MOBA_HEADS):
            qa_ref[h * nq:(h + 1) * nq, :] = q_ref[:, h * MOBA_HD:(h + 1) * MOBA_HD]
        qa = qa_ref[...]
        scores = lax.dot_general(qa, km_ref[...], _NT, precision=HI, preferred_element_type=F32)
        sel_ref[...] = _topk_mask(scores, (lane & (MOBA_KV - 1)) == row_kv, lane, MOBA_TOPK)
        pad_ref[...] = jnp.zeros_like(pad_ref)
        for h in range(MOBA_KV):
            pad_ref[h * nq:(h + 1) * nq, :] = kn_ref[:, h * MOBA_HD:(h + 1) * MOBA_HD]
        logit = lax.dot_general(qa.astype(BF16), pad_ref[...].astype(BF16), _NT, preferred_element_type=F32)
        logit = jnp.where((lane & (nq - 1)) <= (rowi & (nq - 1)), logit, NEG)
        logit = jnp.where((lane >> nq_shift) == row_kv, logit, NEG)
        m0 = jnp.max(logit, axis=-1, keepdims=True)
        p0 = jnp.exp(logit - m0)
        m_ref[...] = m0
        l_ref[...] = jnp.sum(p0, axis=-1, keepdims=True)
        for h in range(MOBA_KV):
            pad_ref[h * nq:(h + 1) * nq, :] = vn_ref[:, h * MOBA_HD:(h + 1) * MOBA_HD]
        acc_ref[...] = jnp.dot(p0.astype(BF16), pad_ref[...].astype(BF16), preferred_element_type=F32)

    qa16 = qa_ref[...].astype(BF16)
    sel = sel_ref[...]
    col = lax.broadcasted_iota(jnp.int32, (rows, prow), 1)
    same_kv = (col & (MOBA_KV - 1)) == (lax.broadcasted_iota(jnp.int32, (rows, prow), 0) >> kv_shift)
    m = m_ref[...]
    l = l_ref[...]
    acc = acc_ref[...]
    for i in range(SAMPLE_BLOCKS_PER_STEP):
        blk = n * SAMPLE_BLOCKS_PER_STEP + i
        gate = jnp.max(jnp.where(lane == blk * MOBA_KV + row_kv, sel, 0.0), axis=-1, keepdims=True) > 0.5
        for half in range(2):
            kpg = kp[2 * i + half][...].astype(BF16)
            vpg = vp[2 * i + half][...].astype(BF16)
            lg = lax.dot_general(qa16, kpg, _NT, preferred_element_type=F32)
            lg = jnp.where(gate, jnp.where(same_kv, lg, NEG), NEG)
            m_new = jnp.maximum(m, jnp.max(lg, axis=-1, keepdims=True))
            alpha = jnp.exp(m - m_new)
            p = jnp.exp(lg - m_new)
            l = alpha * l + jnp.sum(p, axis=-1, keepdims=True)
            acc = alpha * acc + jnp.dot(p.astype(BF16), vpg, preferred_element_type=F32)
            m = m_new
    m_ref[...] = m
    l_ref[...] = l
    acc_ref[...] = acc

    @pl.when(n == nsteps - 1)
    def _():
        out = acc / l
        for h in range(MOBA_HEADS):
            o_ref[:, h * MOBA_HD:(h + 1) * MOBA_HD] = out[h * nq:(h + 1) * nq, :]


def _moba_sample(qn, kn, proj, cache_k, cache_v, page_table, row0, nseq, nq):
    kvw = MOBA_KV * MOBA_HD
    prow = PAGE * MOBA_KV
    n_pages = page_table.shape[1]
    nblk = n_pages * PAGE // MOBA_BLOCK
    nsteps = nblk // SAMPLE_BLOCKS_PER_STEP
    assert nblk * MOBA_KV == 128 and MOBA_BLOCK == 2 * PAGE and MOBA_KV * nq <= 128
    assert nq & (nq - 1) == 0, "new tokens per sequence must be a power of two"
    ck = cache_k.reshape(-1, prow, MOBA_HD)
    cv = cache_v.reshape(-1, prow, MOBA_HD)

    def page(k):
        return pl.BlockSpec((None, prow, MOBA_HD), lambda b, n, pt: (pt[b, n * _PAGES_PER_STEP + k], 0, 0))

    pages = [page(k) for k in range(_PAGES_PER_STEP)]
    kmean = pl.pallas_call(
        _kmean_kernel,
        grid_spec=pltpu.PrefetchScalarGridSpec(
            num_scalar_prefetch=1, grid=(nseq, nsteps),
            in_specs=pages,
            out_specs=pl.BlockSpec((SAMPLE_BLOCKS_PER_STEP * MOBA_KV, MOBA_HD),
                                   lambda b, n, pt: (b * nsteps + n, 0))),
        out_shape=jax.ShapeDtypeStruct((nseq * nblk * MOBA_KV, MOBA_HD), F32),
        compiler_params=_cp("parallel", "parallel"),
        name="moba_kmean",
    )(page_table, *([ck] * _PAGES_PER_STEP))
    rb = row0 // nq
    rows = MOBA_HEADS * nq
    return pl.pallas_call(
        _moba_sample_kernel,
        grid_spec=pltpu.PrefetchScalarGridSpec(
            num_scalar_prefetch=1, grid=(nseq, nsteps),
            in_specs=[pl.BlockSpec((nq, MOBA_HEADS * MOBA_HD), lambda b, n, pt: (rb + b, 0)),
                      pl.BlockSpec((nq, kvw), lambda b, n, pt: (rb + b, 0)),
                      pl.BlockSpec((nq, kvw), lambda b, n, pt: (rb + b, C_V // kvw)),
                      pl.BlockSpec((nblk * MOBA_KV, MOBA_HD), lambda b, n, pt: (b, 0))] + pages + pages,
            out_specs=pl.BlockSpec((nq, MOBA_HEADS * MOBA_HD), lambda b, n, pt: (b, 0)),
            scratch_shapes=[pltpu.VMEM((rows, MOBA_HD), F32),
                            pltpu.VMEM((128, MOBA_HD), F32),
                            pltpu.VMEM((rows, 128), F32),
                            pltpu.VMEM((rows, 1), F32),
                            pltpu.VMEM((rows, 1), F32),
                            pltpu.VMEM((rows, MOBA_HD), F32)]),
        out_shape=jax.ShapeDtypeStruct((nseq * nq, MOBA_HEADS * MOBA_HD), F32),
        compiler_params=_cp("parallel", "arbitrary"),
        name="moba_sample",
    )(page_table, qn, kn, proj, kmean, *([ck] * _PAGES_PER_STEP), *([cv] * _PAGES_PER_STEP))


def _ssd_kernel(*refs, q, qin, nc, has_init):
    if has_init:
        (z_ref, x_ref, b_ref, c_ref, dt_ref, wx_ref, wb_ref, wc_ref, bx_ref, bb_ref, bc_ref, par_ref, gn_ref,
         s0_ref, cx_ref, cb0_ref, cc0_ref, y_ref, so_ref, ext_ref, st_ref, tr_ref) = refs
    else:
        (z_ref, x_ref, b_ref, c_ref, dt_ref, wx_ref, wb_ref, wc_ref, bx_ref, bb_ref, bc_ref, par_ref, gn_ref,
         y_ref, so_ref, ext_ref, st_ref, tr_ref) = refs
    g = pl.program_id(1)
    c = pl.program_id(2)
    gw = SSD_GW
    ns = SSD_STATE
    cw = gw + 2 * ns

    @pl.when(c == 0)
    def _():
        if has_init:
            st_ref[...] = s0_ref[...]
            ext_ref[0:8, 0:gw] = cx_ref[...]
            ext_ref[0:8, gw:gw + ns] = cb0_ref[...]
            ext_ref[0:8, gw + ns:cw] = cc0_ref[...]
        else:
            st_ref[...] = jnp.zeros_like(st_ref)
            ext_ref[0:8, :] = jnp.zeros((8, cw), F32)
        if qin < q:
            ext_ref[8 + qin:8 + q, :] = jnp.zeros((q - qin, cw), F32)

    ext_ref[8:8 + qin, 0:gw] = x_ref[...]
    ext_ref[8:8 + qin, gw:gw + ns] = b_ref[...]
    ext_ref[8:8 + qin, gw + ns:cw] = c_ref[...]
    w = jnp.concatenate([wx_ref[...], wb_ref[...], wc_ref[...]], axis=1)
    bias = jnp.concatenate([bx_ref[...], bb_ref[...], bc_ref[...]], axis=1)
    conv = bias + ext_ref[5:5 + q, :] * w[0:1]
    conv = conv + ext_ref[6:6 + q, :] * w[1:2]
    conv = conv + ext_ref[7:7 + q, :] * w[2:3]
    conv = conv + ext_ref[8:8 + q, :] * w[3:4]
    if nc > 1:
        ext_ref[0:8, :] = ext_ref[qin:qin + 8, :]
    xbc = conv * jax.nn.sigmoid(conv)
    xs = xbc[:, 0:gw]
    bm = xbc[:, gw:gw + ns]
    cm = xbc[:, gw + ns:cw]

    par = par_ref[...]
    dt = jax.nn.softplus(dt_ref[...] + par[0:1, :])
    if qin < q:
        dt = jnp.concatenate([dt, jnp.zeros((q - qin, 128), F32)], axis=0)
    dta = dt * (-jnp.exp(par[1:2, :]))
    ri = lax.broadcasted_iota(jnp.int32, (q, q), 0)
    ci = lax.broadcasted_iota(jnp.int32, (q, q), 1)
    causal = ri >= ci
    acs = jnp.dot(causal.astype(F32), dta, precision=HI, preferred_element_type=F32)
    acs_last = acs[q - 1:q, :]
    to_end = jnp.exp(acs_last - acs) * dt
    eacs = jnp.exp(acs)
    er = lax.broadcasted_iota(jnp.int32, (128, gw), 0)
    ec = lax.broadcasted_iota(jnp.int32, (128, gw), 1)
    expand = (er == g * 8 + (ec >> 6)).astype(F32)
    to_end_x = jnp.dot(to_end, expand, precision=HI, preferred_element_type=F32)
    eacs_x = jnp.dot(eacs, expand, precision=HI, preferred_element_type=F32)
    dskip_x = jnp.dot(jnp.broadcast_to(par[2:3, :], (8, 128)), expand, precision=HI,
                      preferred_element_type=F32)[0:1, :]
    tr_ref[0] = acs.T
    tr_ref[1] = dt.T
    g8 = pl.multiple_of(g * 8, 8)
    acs_t = tr_ref[0, pl.ds(g8, 8), :]
    dt_t = tr_ref[1, pl.ds(g8, 8), :]
    lane = lax.broadcasted_iota(jnp.int32, (q, 128), 1)
    low = lane < SSD_HD

    cb = lax.dot_general(cm.astype(BF16), bm.astype(BF16), _NT, preferred_element_type=F32)
    ys = []
    for pr in range(4):
        ms = []
        for r in (2 * pr, 2 * pr + 1):
            col = jnp.sum(jnp.where(lane == g * 8 + r, acs, 0.0), axis=1, keepdims=True)
            seg = col - acs_t[r:r + 1, :]
            dec = jnp.where(causal, jnp.exp(jnp.minimum(seg, 0.0)), 0.0)
            ms.append((cb * dec * dt_t[r:r + 1, :]).astype(BF16))
        lhs = jnp.concatenate(ms, axis=1)
        xp = xs[:, pr * 128:(pr + 1) * 128]
        rhs = jnp.concatenate([jnp.where(low, xp, 0.0), jnp.where(low, 0.0, xp)], axis=0).astype(BF16)
        ys.append(jnp.dot(lhs, rhs, preferred_element_type=F32))
    y = jnp.concatenate(ys, axis=1)
    st = st_ref[...]
    y = y + lax.dot_general(cm.astype(BF16), st.astype(BF16), _NT, preferred_element_type=F32) * eacs_x
    y = y + dskip_x * xs
    if qin < q:
        y = y[0:qin, :]
    z = z_ref[...]
    y = y * (z * jax.nn.sigmoid(z))
    msq = jnp.mean(y * y, axis=-1, keepdims=True)
    y_ref[...] = ((y * lax.rsqrt(msq + EPS)) * gn_ref[...]).astype(y_ref.dtype)

    xw = (xs * to_end_x).astype(BF16)
    new = lax.dot_general(xw, bm.astype(BF16), _TN, preferred_element_type=F32)
    dr = lax.broadcasted_iota(jnp.int32, (gw, 128), 0)
    dc = lax.broadcasted_iota(jnp.int32, (gw, 128), 1)
    pick = (dc == g * 8 + (dr >> 6)).astype(F32)
    last_t = jnp.exp(tr_ref[0, :, q - 1:q])
    cdec = jnp.dot(pick, jnp.broadcast_to(last_t, (128, ns)), precision=HI, preferred_element_type=F32)
    st_ref[...] = st * cdec + new

    @pl.when(c == nc - 1)
    def _():
        so_ref[...] = st_ref[...]


def _ssd(proj, projdt, row0, nb, nc, q, qin, par, conv_w, conv_b, gn, out_dtype, s0=None, cst=None):
    gw, ns = SSD_GW, SSD_STATE
    rb = row0 // qin
    zb, xb = C_Z // gw, C_XBC // gw
    bb, cb = (C_XBC + D_INNER) // ns, (C_XBC + D_INNER + SSD_GROUPS * ns) // ns
    row = lambda b, g, c: rb + b * nc + c
    in_specs = [
        pl.BlockSpec((qin, gw), lambda b, g, c: (row(b, g, c), zb + g)),
        pl.BlockSpec((qin, gw), lambda b, g, c: (row(b, g, c), xb + g)),
        pl.BlockSpec((qin, ns), lambda b, g, c: (row(b, g, c), bb + g)),
        pl.BlockSpec((qin, ns), lambda b, g, c: (row(b, g, c), cb + g)),
        pl.BlockSpec((qin, 128), lambda b, g, c: (row(b, g, c), 0)),
        pl.BlockSpec((4, gw), lambda b, g, c: (0, g)),
        pl.BlockSpec((4, ns), lambda b, g, c: (0, D_INNER // ns + g)),
        pl.BlockSpec((4, ns), lambda b, g, c: (0, D_INNER // ns + SSD_GROUPS + g)),
        pl.BlockSpec((1, gw), lambda b, g, c: (0, g)),
        pl.BlockSpec((1, ns), lambda b, g, c: (0, D_INNER // ns + g)),
        pl.BlockSpec((1, ns), lambda b, g, c: (0, D_INNER // ns + SSD_GROUPS + g)),
        pl.BlockSpec((8, 128), lambda b, g, c: (0, 0)),
        pl.BlockSpec((1, gw), lambda b, g, c: (0, g)),
    ]
    args = [proj, proj, proj, proj, projdt, conv_w, conv_w, conv_w, conv_b, conv_b, conv_b, par, gn]
    has_init = s0 is not None
    if has_init:
        in_specs += [
            pl.BlockSpec((None, None, gw, ns), lambda b, g, c: (b, g, 0, 0)),
            pl.BlockSpec((None, 8, gw), lambda b, g, c: (b, 0, g)),
            pl.BlockSpec((None, 8, ns), lambda b, g, c: (b, 0, D_INNER // ns + g)),
            pl.BlockSpec((None, 8, ns), lambda b, g, c: (b, 0, D_INNER // ns + SSD_GROUPS + g)),
        ]
        args += [s0, cst, cst, cst]
    return pl.pallas_call(
        functools.partial(_ssd_kernel, q=q, qin=qin, nc=nc, has_init=has_init),
        grid=(nb, SSD_GROUPS, nc),
        in_specs=in_specs,
        out_specs=[pl.BlockSpec((qin, gw), lambda b, g, c: (b * nc + c, g)),
                   pl.BlockSpec((None, None, gw, ns), lambda b, g, c: (b, g, 0, 0))],
        out_shape=[jax.ShapeDtypeStruct((nb * nc * qin, D_INNER), out_dtype),
                   jax.ShapeDtypeStruct((nb, SSD_GROUPS, gw, ns), F32)],
        scratch_shapes=[pltpu.VMEM((q + 8, gw + 2 * ns), F32),
                        pltpu.VMEM((gw, ns), F32),
                        pltpu.VMEM((2, 128, q), F32)],
        compiler_params=_cp("parallel", "parallel", "arbitrary"),
        name="ssd",
    )(*args)


def _mem_attn_kernel(q_ref, k_ref, v_ref, o_ref):
    s = lax.dot_general(q_ref[...].astype(BF16), k_ref[...].astype(BF16), _NT, preferred_element_type=F32)
    m = jnp.max(s, axis=-1, keepdims=True)
    p = jnp.exp(s - m)
    l = jnp.sum(p, axis=-1, keepdims=True)
    o = jnp.dot(p.astype(BF16), v_ref[...].astype(BF16), preferred_element_type=F32) / l
    o_ref[...] = o.astype(o_ref.dtype)


def _mem_attn(qmn, row0, nb, seq, tq, k2d, k_cb, v2d, v_cb, out_dtype):
    nt = seq // tq
    rb = row0 // tq
    return pl.pallas_call(
        _mem_attn_kernel,
        grid=(nb, nt, MEM_HEADS),
        in_specs=[pl.BlockSpec((tq, MEM_HD), lambda b, i, h: (rb + b * nt + i, h)),
                  pl.BlockSpec((MEM_TOKENS, MEM_HD), lambda b, i, h: (b, k_cb + h)),
                  pl.BlockSpec((MEM_TOKENS, MEM_HD), lambda b, i, h: (b, v_cb + h))],
        out_specs=pl.BlockSpec((tq, MEM_HD), lambda b, i, h: (b * nt + i, h)),
        out_shape=jax.ShapeDtypeStruct((nb * seq, MEM_HEADS * MEM_HD), out_dtype),
        compiler_params=_cp("parallel", "parallel", "parallel"),
        name="mem_attn",
    )(qmn, k2d, v2d)


def _merge_kernel(a_ref, y_ref, m_ref, wa_ref, wb_ref, wc_ref, g0_ref, g1_ref, g2_ref,
                  b0_ref, b1_ref, b2_ref, o_ref):
    oa = jnp.dot(a_ref[...], wa_ref[...], preferred_element_type=F32)
    ob = jnp.dot(y_ref[...], wb_ref[...], preferred_element_type=F32)
    oc = jnp.dot(m_ref[...], wc_ref[...], preferred_element_type=F32)
    o = jax.nn.sigmoid(g0_ref[...] + b0_ref[...]) * oa
    o = o + jax.nn.sigmoid(g1_ref[...] + b1_ref[...]) * ob
    o = o + jax.nn.sigmoid(g2_ref[...] + b2_ref[...]) * oc
    o_ref[...] = o.astype(o_ref.dtype)


def _merge(attn, yssd, memo, wa, wb, wc, projb, bgate, tm=528, tn=512):
    m = attn.shape[0]
    nj = D_MODEL // tn
    gcb = CB_GATE // tn
    act = lambda k: pl.BlockSpec((tm, k), lambda i, j: (i, 0))
    wsp = lambda k: pl.BlockSpec((k, tn), lambda i, j: (0, j))
    gsp = lambda br: pl.BlockSpec((tm, tn), lambda i, j: (i, gcb + br * nj + j))
    bsp = lambda br: pl.BlockSpec((1, tn), lambda i, j: (0, br * nj + j))
    return pl.pallas_call(
        _merge_kernel,
        grid=(m // tm, nj),
        in_specs=[act(attn.shape[1]), act(yssd.shape[1]), act(memo.shape[1]),
                  wsp(wa.shape[0]), wsp(wb.shape[0]), wsp(wc.shape[0]),
                  gsp(0), gsp(1), gsp(2), bsp(0), bsp(1), bsp(2)],
        out_specs=pl.BlockSpec((tm, tn), lambda i, j: (i, j)),
        out_shape=jax.ShapeDtypeStruct((m, D_MODEL), BF16),
        compiler_params=_cp("parallel", "parallel"),
        name="merge",
    )(attn, yssd, memo, wa, wb, wc, projb, projb, projb, bgate, bgate, bgate)


def _router_kernel(x_ref, g_ref, w_ref, b_ref, h_ref, idx_ref, p_ref):
    x = x_ref[...]
    ms = jnp.mean(x * x, axis=-1, keepdims=True)
    h = (x * lax.rsqrt(ms + EPS)) * g_ref[...]
    h_ref[...] = h
    logits = jnp.dot(h, w_ref[...], precision=HI, preferred_element_type=F32) + b_ref[...]
    lane = lax.broadcasted_iota(jnp.int32, logits.shape, 1)
    lanef = lane.astype(F32)
    s = jnp.where(lane < N_EXPERTS, logits, -jnp.inf)
    vals, idxs = [], []
    for _ in range(TOP_K):
        m = jnp.max(s, axis=-1, keepdims=True)
        idx = jnp.min(jnp.where(s == m, lanef, 128.0), axis=-1, keepdims=True)
        vals.append(m)
        idxs.append(idx)
        s = jnp.where(lanef == idx, -jnp.inf, s)
    es = [jnp.exp(v - vals[0]) for v in vals]
    den = es[0] + es[1] + es[2] + es[3]
    p_out = jnp.zeros(logits.shape, F32)
    i_out = jnp.zeros(logits.shape, F32)
    for k in range(TOP_K):
        p_out = jnp.where(lane == k, es[k] / den, p_out)
        i_out = jnp.where(lane == k, idxs[k], i_out)
    p_ref[...] = p_out
    idx_ref[...] = i_out.astype(jnp.int32)


def _router(x2, g, w_router, b_router, tr=264):
    m, d = x2.shape
    wr = jnp.zeros((d, 128), F32).at[:, :N_EXPERTS].set(w_router)
    br = jnp.zeros((1, 128), F32).at[0, :N_EXPERTS].set(b_router)
    return pl.pallas_call(
        _router_kernel,
        grid=(m // tr,),
        in_specs=[pl.BlockSpec((tr, d), lambda i: (i, 0)),
                  pl.BlockSpec((1, d), lambda i: (0, 0)),
                  pl.BlockSpec((d, 128), lambda i: (0, 0)),
                  pl.BlockSpec((1, 128), lambda i: (0, 0))],
        out_specs=[pl.BlockSpec((tr, d), lambda i: (i, 0)),
                   pl.BlockSpec((tr, 128), lambda i: (i, 0)),
                   pl.BlockSpec((tr, 128), lambda i: (i, 0))],
        out_shape=[jax.ShapeDtypeStruct((m, d), F32),
                   jax.ShapeDtypeStruct((m, 128), jnp.int32),
                   jax.ShapeDtypeStruct((m, 128), F32)],
        compiler_params=_cp("parallel"),
        name="router",
    )(x2, g.reshape(1, d), wr, br)


MOE_TM = 256
MOE_TN_UP = 512
MOE_TN_DOWN = 1024
COMBINE_TOKENS = 64
DMA_UNROLL = 8


def _dispatch(ridx, rprob, tm, n_tiles):
    t = ridx.shape[0]
    na = t * TOP_K
    e_flat = ridx.reshape(na)
    experts = jnp.arange(N_EXPERTS, dtype=jnp.int32)
    counts = jnp.sum(e_flat[:, None] == experts[None, :], axis=0, dtype=jnp.int32)
    nt = (counts + tm - 1) // tm
    t_end = jnp.cumsum(nt)
    t0 = t_end - nt
    n_used = t_end[-1]
    start = jnp.cumsum(counts) - counts
    order = jnp.argsort(e_flat, stable=True).astype(jnp.int32)
    inv = jnp.argsort(order).astype(jnp.int32)
    pos = t0[e_flat] * tm + (inv - start[e_flat])
    tiles = jnp.arange(n_tiles, dtype=jnp.int32)
    tile_e = jnp.minimum(jnp.sum(t_end[None, :] <= tiles[:, None], axis=1, dtype=jnp.int32), N_EXPERTS - 1)
    rho = jnp.arange(n_tiles * tm, dtype=jnp.int32)
    e_r = tile_e[rho // tm]
    rank = rho - t0[e_r] * tm
    valid = (rank < counts[e_r]) & (rho // tm < n_used)
    a = order[jnp.clip(start[e_r] + rank, 0, na - 1)]
    tok = jnp.where(valid, a // TOP_K, 0)
    prob = jnp.where(valid, rprob.reshape(na)[a], 0.0)
    return tok, prob.reshape(-1, 1), pos, (nt, t0, tile_e, n_used)


def _schedule(tables, n_tiles, nj):
    nt, t0, tile_e, n_used = tables
    s = jnp.arange(n_tiles * nj, dtype=jnp.int32)
    valid = (s < nj * n_used).astype(jnp.int32)
    sc = jnp.minimum(s, nj * n_used - 1)
    e = tile_e[sc // nj]
    local = sc - nj * t0[e]
    sj = local // nt[e]
    si = t0[e] + local % nt[e]
    first = ((local % nt[e]) == 0).astype(jnp.int32)
    so_i = jnp.where(valid == 1, si, s // nj)
    so_j = jnp.where(valid == 1, sj, s % nj)
    return e, sj, si, first, valid, so_i, so_j


def _row_gather_pipeline(n_rows, issue_row, wait_row, step, n_steps, active):
    def issue(st, slot):
        def body(gi, carry):
            for u in range(DMA_UNROLL):
                issue_row(st, slot, gi * DMA_UNROLL + u)
            return carry
        lax.fori_loop(0, n_rows // DMA_UNROLL, body, 0)

    def wait(slot):
        def body(gi, carry):
            for u in range(DMA_UNROLL):
                wait_row(slot, gi * DMA_UNROLL + u)
            return carry
        lax.fori_loop(0, n_rows // DMA_UNROLL, body, 0)

    @pl.when(jnp.logical_and(step == 0, active(0)))
    def _():
        issue(0, 0)

    @pl.when(jnp.logical_and(step + 1 < n_steps, active(step + 1)))
    def _():
        issue(step + 1, (step + 1) & 1)

    @pl.when(active(step))
    def _():
        wait(step & 1)


def _gather_kernel(tok_ref, nused_ref, h_hbm, o_ref, buf_ref, sem):
    i = pl.program_id(0)
    rows = buf_ref.shape[1]

    def copy(slot, r, src_row):
        return pltpu.make_async_copy(h_hbm.at[pl.ds(src_row, 1), :], buf_ref.at[slot, pl.ds(r, 1), :],
                                     sem.at[slot])

    _row_gather_pipeline(
        rows,
        lambda st, slot, r: copy(slot, r, tok_ref[st * rows + r]).start(),
        lambda slot, r: copy(slot, r, 0).wait(),
        i, pl.num_programs(0), lambda st: st < nused_ref[0])

    @pl.when(i < nused_ref[0])
    def _():
        o_ref[...] = buf_ref[i & 1].astype(o_ref.dtype)

    @pl.when(i >= nused_ref[0])
    def _():
        o_ref[...] = jnp.zeros_like(o_ref)


def _gather_rows(tok, n_used, h2, rows):
    total = tok.shape[0]
    d = h2.shape[1]
    return pl.pallas_call(
        _gather_kernel,
        grid_spec=pltpu.PrefetchScalarGridSpec(
            num_scalar_prefetch=2, grid=(total // rows,),
            in_specs=[pl.BlockSpec(memory_space=pl.ANY)],
            out_specs=pl.BlockSpec((rows, d), lambda i, tok, nu: (i, 0)),
            scratch_shapes=[pltpu.VMEM((2, rows, d), F32), pltpu.SemaphoreType.DMA((2,))]),
        out_shape=jax.ShapeDtypeStruct((total, d), BF16),
        compiler_params=_cp("arbitrary"),
        name="moe_gather",
    )(tok, n_used.reshape(1), h2)


def _moe_up_kernel(se, sj, si, sfirst, svalid, so_i, so_j, x_ref, wg_ref, wu_ref, bg_ref, bu_ref, o_ref, wg16, wu16):
    s = pl.program_id(0)

    @pl.when(svalid[s] == 1)
    def _():
        @pl.when(sfirst[s] == 1)
        def _():
            wg16[...] = wg_ref[...].astype(BF16)
            wu16[...] = wu_ref[...].astype(BF16)

        x = x_ref[...]
        g = jnp.minimum(jnp.dot(x, wg16[...], preferred_element_type=F32) + bg_ref[...], SWIGLU_LIMIT)
        u = jnp.clip(jnp.dot(x, wu16[...], preferred_element_type=F32) + bu_ref[...], -SWIGLU_LIMIT, SWIGLU_LIMIT)
        act = (u + 1.0) * (g * jax.nn.sigmoid(SWIGLU_ALPHA * g))
        o_ref[...] = act.astype(o_ref.dtype)

    @pl.when(svalid[s] == 0)
    def _():
        o_ref[...] = jnp.zeros_like(o_ref)


def _moe_down_kernel(se, sj, si, sfirst, svalid, so_i, so_j, a_ref, w_ref, b_ref, p_ref, o_ref, w16):
    s = pl.program_id(0)

    @pl.when(svalid[s] == 1)
    def _():
        @pl.when(sfirst[s] == 1)
        def _():
            w16[...] = w_ref[...].astype(BF16)

        y = jnp.dot(a_ref[...], w16[...], preferred_element_type=F32) + b_ref[...]
        o_ref[...] = p_ref[...] * y

    @pl.when(svalid[s] == 0)
    def _():
        o_ref[...] = jnp.zeros_like(o_ref)


def _moe_up(sched, xs, wg, wu, bg, bu, tm, tn):
    rows, d = xs.shape
    f = wg.shape[2]
    nsteps = sched[0].shape[0]
    xsp = pl.BlockSpec((tm, d), lambda s, se, sj, si, sf, sv, oi, oj: (si[s], 0))
    wsp = pl.BlockSpec((None, d, tn), lambda s, se, sj, si, sf, sv, oi, oj: (se[s], 0, sj[s]))
    bsp = pl.BlockSpec((None, 1, tn), lambda s, se, sj, si, sf, sv, oi, oj: (se[s], 0, sj[s]))
    return pl.pallas_call(
        _moe_up_kernel,
        grid_spec=pltpu.PrefetchScalarGridSpec(
            num_scalar_prefetch=7, grid=(nsteps,),
            in_specs=[xsp, wsp, wsp, bsp, bsp],
            out_specs=pl.BlockSpec((tm, tn), lambda s, se, sj, si, sf, sv, oi, oj: (oi[s], oj[s])),
            scratch_shapes=[pltpu.VMEM((d, tn), BF16), pltpu.VMEM((d, tn), BF16)]),
        out_shape=jax.ShapeDtypeStruct((rows, f), BF16),
        compiler_params=_cp("arbitrary"),
        name="moe_up",
    )(*sched, xs, wg, wu, bg, bu)


def _moe_down(sched, act, wd, bd, prob, tm, tn):
    rows, f = act.shape
    d = wd.shape[2]
    nsteps = sched[0].shape[0]
    return pl.pallas_call(
        _moe_down_kernel,
        grid_spec=pltpu.PrefetchScalarGridSpec(
            num_scalar_prefetch=7, grid=(nsteps,),
            in_specs=[pl.BlockSpec((tm, f), lambda s, se, sj, si, sf, sv, oi, oj: (si[s], 0)),
                      pl.BlockSpec((None, f, tn), lambda s, se, sj, si, sf, sv, oi, oj: (se[s], 0, sj[s])),
                      pl.BlockSpec((None, 1, tn), lambda s, se, sj, si, sf, sv, oi, oj: (se[s], 0, sj[s])),
                      pl.BlockSpec((tm, 1), lambda s, se, sj, si, sf, sv, oi, oj: (si[s], 0))],
            out_specs=pl.BlockSpec((tm, tn), lambda s, se, sj, si, sf, sv, oi, oj: (oi[s], oj[s])),
            scratch_shapes=[pltpu.VMEM((f, tn), BF16)]),
        out_shape=jax.ShapeDtypeStruct((rows, d), F32),
        compiler_params=_cp("arbitrary"),
        name="moe_down",
    )(*sched, act, wd, bd, prob)


def _combine_kernel(pos_ref, x_ref, y_hbm, o_ref, buf_ref, sem):
    i = pl.program_id(0)
    n = COMBINE_TOKENS * TOP_K

    def copy(slot, r, src_row):
        return pltpu.make_async_copy(y_hbm.at[pl.ds(src_row, 1), :],
                                     buf_ref.at[slot, r & (TOP_K - 1), pl.ds(r >> 2, 1), :], sem.at[slot])

    _row_gather_pipeline(
        n,
        lambda st, slot, r: copy(slot, r, pos_ref[st * n + r]).start(),
        lambda slot, r: copy(slot, r, 0).wait(),
        i, pl.num_programs(0), lambda st: st >= 0)
    slot = i & 1
    o_ref[...] = x_ref[...] + ((buf_ref[slot, 0] + buf_ref[slot, 1]) + (buf_ref[slot, 2] + buf_ref[slot, 3]))


def _combine(pos, x2, ysort):
    t, d = x2.shape
    assert TOP_K == 4
    return pl.pallas_call(
        _combine_kernel,
        grid_spec=pltpu.PrefetchScalarGridSpec(
            num_scalar_prefetch=1, grid=(t // COMBINE_TOKENS,),
            in_specs=[pl.BlockSpec((COMBINE_TOKENS, d), lambda i, pos: (i, 0)),
                      pl.BlockSpec(memory_space=pl.ANY)],
            out_specs=pl.BlockSpec((COMBINE_TOKENS, d), lambda i, pos: (i, 0)),
            scratch_shapes=[pltpu.VMEM((2, TOP_K, COMBINE_TOKENS, d), F32), pltpu.SemaphoreType.DMA((2,))]),
        out_shape=jax.ShapeDtypeStruct((t, d), F32),
        compiler_params=_cp("arbitrary"),
        name="moe_combine",
    )(pos, x2, ysort)


def _moe(x2, norm_g, w_router, b_router, wg, bg, wu, bu, wd, bd):
    t = x2.shape[0]
    h2, ridx, rprob = _router(x2, norm_g, w_router, b_router)
    n_tiles = (t * TOP_K) // MOE_TM + N_EXPERTS
    tok, prob, pos, tables = _dispatch(ridx[:, :TOP_K], rprob[:, :TOP_K], MOE_TM, n_tiles)
    xs = _gather_rows(tok, tables[3], h2, MOE_TM)
    act = _moe_up(_schedule(tables, n_tiles, D_FF // MOE_TN_UP), xs, wg, wu,
                  bg.reshape(N_EXPERTS, 1, D_FF), bu.reshape(N_EXPERTS, 1, D_FF), MOE_TM, MOE_TN_UP)
    ysort = _moe_down(_schedule(tables, n_tiles, D_MODEL // MOE_TN_DOWN), act, wd,
                      bd.reshape(N_EXPERTS, 1, D_MODEL), prob, MOE_TM, MOE_TN_DOWN)
    return _combine(pos, x2, ysort)


def kernel(x_prompt, x_sample, mem_prompt, cache_k, cache_v, page_table, state_ssm, state_conv, cache_mem_k, cache_mem_v, norm_mix_g, w_in, b_branch_gate, moba_q_norm_g, moba_k_norm_g, conv_w, conv_b, dt_bias, a_log, d_skip, ssd_norm_g, mem_norm_g, w_mem_kv, mem_q_norm_g, mem_k_norm_g, w_br_moba, w_br_ssd, w_br_mem, w_out, norm_ffn_g, w_router, b_router, w_exp_gate, b_exp_gate, w_exp_up, b_exp_up, w_exp_down, b_exp_down):
    assert w_in.shape[0] == 1, "single layer"
    nb, seq, d = x_prompt.shape
    ns, nq, _ = x_sample.shape
    tp = nb * seq
    x_all = jnp.concatenate([x_prompt.reshape(tp, d), x_sample.reshape(ns * nq, d)], axis=0)

    w_in2 = w_in.reshape(d, -1)
    h = _rmsnorm(x_all, norm_mix_g[0], BF16)
    proja = _matmul_f32w(h, w_in2, W_IN_A, tm=1056, tn=512, name="in_proj_a")
    w_dt = jnp.pad(w_in2[:, W_IN_A:W_IN_B], ((0, 0), (0, 128 - SSD_HEADS))).astype(BF16)
    projdt = _matmul(h, w_dt, tm=1056, tn=128, name="in_proj_dt")
    projb = _matmul(h, w_in2[:, W_IN_B:].astype(BF16), tm=1056, tn=512, name="in_proj_b")

    qn = _head_norm(proja, C_Q, MOBA_HEADS * MOBA_HD, MOBA_HD, moba_q_norm_g[0], MOBA_HD ** -0.5)
    kn = _head_norm(proja, C_K, MOBA_KV * MOBA_HD, MOBA_HD, moba_k_norm_g[0], 1.0)
    qmn = _head_norm(projb, CB_QM, MEM_HEADS * MEM_HD, MEM_HD, mem_q_norm_g[0], MEM_HD ** -0.5)

    attn_p = _moba_prompt(qn, kn, proja, nb, seq)
    attn_s = _moba_sample(qn, kn, proja, cache_k, cache_v, page_table, tp, ns, nq)
    attn = jnp.concatenate([attn_p, attn_s.astype(BF16)], axis=0)

    par = jnp.zeros((8, 128), F32)
    par = par.at[0, :SSD_HEADS].set(dt_bias[0]).at[1, :SSD_HEADS].set(a_log[0]).at[2, :SSD_HEADS].set(d_skip[0])
    gn = ssd_norm_g.reshape(1, D_INNER)
    cw2 = conv_w.reshape(-1, CONV_CH)
    cb2 = conv_b.reshape(1, CONV_CH)
    y_p, st_p = _ssd(proja, projdt, 0, nb, seq // 256, 256, 256, par, cw2, cb2, gn, BF16)
    s0 = state_ssm.reshape(ns, SSD_GROUPS, SSD_GW, SSD_STATE)
    cst = jnp.concatenate([jnp.zeros((ns, 5, CONV_CH), F32), state_conv.reshape(ns, 3, CONV_CH)], axis=1)
    y_s, st_s = _ssd(proja, projdt, tp, ns, 1, 128, nq, par, cw2, cb2, gn, F32, s0=s0, cst=cst)
    yssd = jnp.concatenate([y_p, y_s.astype(BF16)], axis=0)

    hm = _rmsnorm(mem_prompt.reshape(nb * MEM_TOKENS, d), mem_norm_g[0], BF16)
    memkv = _matmul(hm, w_mem_kv.reshape(d, -1).astype(BF16), tm=512, tn=512, name="mem_kv")
    mw = MEM_HEADS * MEM_HD
    mk = _head_norm(memkv, 0, mw, MEM_HD, mem_k_norm_g[0], 1.0, tr=256)
    memo_p = _mem_attn(qmn, 0, nb, seq, 256, mk, 0, memkv, MEM_HEADS, BF16)
    cmk = cache_mem_k.reshape(ns * MEM_TOKENS, mw)
    cmv = cache_mem_v.reshape(ns * MEM_TOKENS, mw)
    memo_s = _mem_attn(qmn, tp, ns, nq, nq, cmk, 0, cmv, 0, F32)
    memo = jnp.concatenate([memo_p, memo_s.astype(BF16)], axis=0)

    merged = _merge(attn, yssd, memo, w_br_moba.reshape(-1, d).astype(BF16), w_br_ssd.reshape(-1, d).astype(BF16),
                    w_br_mem.reshape(-1, d).astype(BF16), projb, b_branch_gate.reshape(1, -1))
    x2 = _matmul(merged, w_out.reshape(d, d).astype(BF16), tm=1056, tn=512, residual=x_all, name="out_proj")

    y = _moe(x2, norm_ffn_g[0], w_router.reshape(d, N_EXPERTS), b_router[0],
             w_exp_gate.reshape(N_EXPERTS, d, D_FF), b_exp_gate.reshape(N_EXPERTS, D_FF),
             w_exp_up.reshape(N_EXPERTS, d, D_FF), b_exp_up.reshape(N_EXPERTS, D_FF),
             w_exp_down.reshape(N_EXPERTS, D_FF, d), b_exp_down.reshape(N_EXPERTS, d))

    kvw = MOBA_KV * MOBA_HD
    v_all = proja[:, C_V:C_V + kvw]
    conv_p = jnp.stack([proja[(b + 1) * seq - 3:(b + 1) * seq, C_XBC:C_XBC + CONV_CH] for b in range(nb)])
    conv_s = proja[tp:, C_XBC:C_XBC + CONV_CH].reshape(ns, nq, CONV_CH)[:, nq - 3:]
    return (
        y[:tp].reshape(nb, seq, d),
        y[tp:].reshape(ns, nq, d),
        kn[:tp].reshape(1, nb, seq, MOBA_KV, MOBA_HD),
        v_all[:tp].reshape(1, nb, seq, MOBA_KV, MOBA_HD),
        kn[tp:].reshape(1, ns, nq, MOBA_KV, MOBA_HD),
        v_all[tp:].reshape(1, ns, nq, MOBA_KV, MOBA_HD),
        st_p.reshape(1, nb, SSD_HEADS, SSD_HD, SSD_STATE),
        conv_p[None],
        st_s.reshape(1, ns, SSD_HEADS, SSD_HD, SSD_STATE),
        conv_s[None],
        mk.reshape(1, nb, MEM_TOKENS, MEM_HEADS, MEM_HD),
        memkv[:, mw:].reshape(1, nb, MEM_TOKENS, MEM_HEADS, MEM_HD),
    )
```

```python
import functools

import jax
import jax.numpy as jnp
from jax import lax
from jax.experimental import pallas as pl
from jax.experimental.pallas import tpu as pltpu

F32 = jnp.float32
BF16 = jnp.bfloat16
HI = lax.Precision.HIGHEST

D_MODEL = 4096
EPS = 1e-6
NEG = -1e30

MOBA_HEADS = 16
MOBA_KV = 4
MOBA_GROUP = 4
MOBA_HD = 128
MOBA_BLOCK = 256
MOBA_TOPK = 3
PAGE = 128
D_INNER = 4096
SSD_HEADS = 64
SSD_HD = 64
SSD_GROUPS = 8
SSD_STATE = 128
SSD_GW = D_INNER // SSD_GROUPS
CONV_CH = D_INNER + 2 * SSD_GROUPS * SSD_STATE
MEM_TOKENS = 256
MEM_HEADS = 4
MEM_HD = 512
N_EXPERTS = 32
TOP_K = 4
D_FF = 4096
SWIGLU_LIMIT = 7.0
SWIGLU_ALPHA = 1.702

C_Q = 0
C_K = 2048
C_V = 2560
C_Z = 3072
C_XBC = 7168
W_IN_A = 13312
W_IN_B = 13376
CB_QM = 0
CB_GATE = 2048

V7X_VMEM_LIMIT = 56 * 1024 * 1024

_NT = (((1,), (1,)), ((), ()))
_TN = (((0,), (0,)), ((), ()))


def _cp(*sem):
    return pltpu.CompilerParams(dimension_semantics=sem, vmem_limit_bytes=V7X_VMEM_LIMIT)


def _rmsnorm_kernel(x_ref, g_ref, o_ref):
    x = x_ref[...]
    ms = jnp.mean(x * x, axis=-1, keepdims=True)
    o_ref[...] = ((x * lax.rsqrt(ms + EPS)) * g_ref[...]).astype(o_ref.dtype)


def _rmsnorm(x, g, out_dtype, tr=256):
    m, d = x.shape
    return pl.pallas_call(
        _rmsnorm_kernel,
        grid=(m // tr,),
        in_specs=[pl.BlockSpec((tr, d), lambda i: (i, 0)), pl.BlockSpec((1, d), lambda i: (0, 0))],
        out_specs=pl.BlockSpec((tr, d), lambda i: (i, 0)),
        out_shape=jax.ShapeDtypeStruct((m, d), out_dtype),
        compiler_params=_cp("parallel"),
        name="rmsnorm",
    )(x, g.reshape(1, d))


def _head_norm_kernel(x_ref, g_ref, o_ref, *, hd, scale):
    x = x_ref[...]
    g = g_ref[...]
    for s in range(x.shape[1] // hd):
        xs = x[:, s * hd:(s + 1) * hd]
        ms = jnp.mean(xs * xs, axis=-1, keepdims=True)
        y = (xs * lax.rsqrt(ms + EPS)) * g[:, s * hd:(s + 1) * hd]
        if scale != 1.0:
            y = y * scale
        o_ref[:, s * hd:(s + 1) * hd] = y


def _head_norm(src, col_off, width, hd, gain, scale, tr=264):
    rows = src.shape[0]
    cb = col_off // 512
    return pl.pallas_call(
        functools.partial(_head_norm_kernel, hd=hd, scale=scale),
        grid=(rows // tr, width // 512),
        in_specs=[pl.BlockSpec((tr, 512), lambda i, j: (i, cb + j)),
                  pl.BlockSpec((1, 512), lambda i, j: (0, 0))],
        out_specs=pl.BlockSpec((tr, 512), lambda i, j: (i, j)),
        out_shape=jax.ShapeDtypeStruct((rows, width), F32),
        compiler_params=_cp("parallel", "parallel"),
        name="head_norm",
    )(src, jnp.tile(gain, 512 // hd).reshape(1, 512))


def _mm_kernel(a_ref, w_ref, o_ref):
    o_ref[...] = jnp.dot(a_ref[...], w_ref[...], preferred_element_type=F32).astype(o_ref.dtype)


def _mm_res_kernel(a_ref, w_ref, r_ref, o_ref):
    o_ref[...] = r_ref[...] + jnp.dot(a_ref[...], w_ref[...], preferred_element_type=F32)


def _matmul(a, w, tm, tn, out_dtype=F32, residual=None, name="matmul"):
    m, k = a.shape
    n = w.shape[1]
    in_specs = [pl.BlockSpec((tm, k), lambda i, j: (i, 0)), pl.BlockSpec((k, tn), lambda i, j: (0, j))]
    args = [a, w]
    body = _mm_kernel
    if residual is not None:
        in_specs.append(pl.BlockSpec((tm, tn), lambda i, j: (i, j)))
        args.append(residual)
        body = _mm_res_kernel
    return pl.pallas_call(
        body,
        grid=(m // tm, n // tn),
        in_specs=in_specs,
        out_specs=pl.BlockSpec((tm, tn), lambda i, j: (i, j)),
        out_shape=jax.ShapeDtypeStruct((m, n), out_dtype),
        compiler_params=_cp("parallel", "parallel"),
        name=name,
    )(*args)


W_CHUNK = 64


def _in_proj_kernel(a_ref, *refs, nchunks):
    w_refs = refs[:nchunks]
    o_ref, w16_ref = refs[nchunks:]

    @pl.when(pl.program_id(1) == 0)
    def _():
        for k, w in enumerate(w_refs):
            w16_ref[k * W_CHUNK:(k + 1) * W_CHUNK, :] = w[...].astype(BF16)
        if nchunks * W_CHUNK < w16_ref.shape[0]:
            w16_ref[nchunks * W_CHUNK:, :] = jnp.zeros((w16_ref.shape[0] - nchunks * W_CHUNK, w16_ref.shape[1]), BF16)

    o_ref[...] = lax.dot_general(a_ref[...], w16_ref[...], _NT, preferred_element_type=F32)


def _in_proj(a, wt, row0, n, tm, tn, name):
    m, k = a.shape
    nchunks = min(n, tn) // W_CHUNK
    ntiles = max(n // tn, 1)
    assert row0 % W_CHUNK == 0 and (n % tn == 0 or n < tn) and n % W_CHUNK == 0
    wspec = lambda c: pl.BlockSpec((W_CHUNK, k), lambda j, i: (row0 // W_CHUNK + j * nchunks + c, 0))
    return pl.pallas_call(
        functools.partial(_in_proj_kernel, nchunks=nchunks),
        grid=(ntiles, m // tm),
        in_specs=[pl.BlockSpec((tm, k), lambda j, i: (i, 0))] + [wspec(c) for c in range(nchunks)],
        out_specs=pl.BlockSpec((tm, tn), lambda j, i: (i, j)),
        out_shape=jax.ShapeDtypeStruct((m, ntiles * tn), F32),
        scratch_shapes=[pltpu.VMEM((tn, k), BF16)],
        compiler_params=_cp("parallel", "arbitrary"),
        name=name,
    )(a, *([wt] * nchunks))


def _topk_mask(s, valid, lane, k):
    lanef = lane.astype(F32)
    s = jnp.where(valid, s, -jnp.inf)
    sel = jnp.zeros(s.shape, F32)
    for _ in range(k):
        m = jnp.max(s, axis=-1, keepdims=True)
        idx = jnp.min(jnp.where(s == m, lanef, 128.0), axis=-1, keepdims=True)
        hit = lanef == idx
        sel = jnp.where(hit, 1.0, sel)
        s = jnp.where(hit, -jnp.inf, s)
    return jnp.where(valid, sel, 0.0)


def _moba_prompt_kernel(q_ref, k_ref, v_ref, o_ref, kmean_ref, acc_ref):
    qb = pl.program_id(2)
    blk = MOBA_BLOCK
    grp = MOBA_GROUP
    rows = grp * blk
    nblk = k_ref.shape[0] // blk

    @pl.when(qb == 0)
    def _():
        kmean_ref[...] = jnp.zeros_like(kmean_ref)
        for n in range(nblk):
            kmean_ref[n:n + 1, :] = jnp.sum(k_ref[n * blk:(n + 1) * blk, :], axis=0, keepdims=True) * (1.0 / blk)

    q4 = q_ref[...]
    q = jnp.concatenate([q4[:, g * MOBA_HD:(g + 1) * MOBA_HD] for g in range(grp)], axis=0)
    scores = lax.dot_general(q, kmean_ref[...], _NT, precision=HI, preferred_element_type=F32)
    lane = lax.broadcasted_iota(jnp.int32, (rows, 128), 1)
    sel = _topk_mask(scores, lane < qb, lane, MOBA_TOPK)
    q16 = q.astype(BF16)

    own = pl.multiple_of(qb * blk, blk)
    k_own = k_ref[pl.ds(own, blk), :].astype(BF16)
    v_own = v_ref[pl.ds(own, blk), :].astype(BF16)
    logit = lax.dot_general(q16, k_own, _NT, preferred_element_type=F32)
    t_row = lax.broadcasted_iota(jnp.int32, (rows, blk), 0) & (blk - 1)
    j_col = lax.broadcasted_iota(jnp.int32, (rows, blk), 1)
    logit = jnp.where(j_col <= t_row, logit, NEG)
    m0 = jnp.max(logit, axis=-1, keepdims=True)
    p0 = jnp.exp(logit - m0)
    l0 = jnp.sum(p0, axis=-1, keepdims=True)
    acc_ref[...] = jnp.dot(p0.astype(BF16), v_own, preferred_element_type=F32)

    def body(n, carry):
        m, l = carry
        gate = jnp.max(jnp.where(lane == n, sel, 0.0), axis=-1, keepdims=True) > 0.5
        off = pl.multiple_of(n * blk, blk)
        kb = k_ref[pl.ds(off, blk), :].astype(BF16)
        vb = v_ref[pl.ds(off, blk), :].astype(BF16)
        lg = lax.dot_general(q16, kb, _NT, preferred_element_type=F32)
        lg = jnp.where(gate, lg, NEG)
        m_new = jnp.maximum(m, jnp.max(lg, axis=-1, keepdims=True))
        alpha = jnp.exp(m - m_new)
        p = jnp.exp(lg - m_new)
        l = alpha * l + jnp.sum(p, axis=-1, keepdims=True)
        acc_ref[...] = alpha * acc_ref[...] + jnp.dot(p.astype(BF16), vb, preferred_element_type=F32)
        return m_new, l

    _, l = lax.fori_loop(0, qb, body, (m0, l0))
    out = acc_ref[...] / l
    for g in range(grp):
        o_ref[:, g * MOBA_HD:(g + 1) * MOBA_HD] = out[g * blk:(g + 1) * blk, :].astype(o_ref.dtype)


def _moba_prompt(qn, kn, proj, nb, seq):
    nqb = seq // MOBA_BLOCK
    gw = MOBA_GROUP * MOBA_HD
    vcb = C_V // MOBA_HD
    return pl.pallas_call(
        _moba_prompt_kernel,
        grid=(nb, MOBA_KV, nqb),
        in_specs=[pl.BlockSpec((MOBA_BLOCK, gw), lambda b, h, i: (b * nqb + i, h)),
                  pl.BlockSpec((seq, MOBA_HD), lambda b, h, i: (b, h)),
                  pl.BlockSpec((seq, MOBA_HD), lambda b, h, i: (b, vcb + h))],
        out_specs=pl.BlockSpec((MOBA_BLOCK, gw), lambda b, h, i: (b * nqb + i, h)),
        out_shape=jax.ShapeDtypeStruct((nb * seq, MOBA_HEADS * MOBA_HD), BF16),
        scratch_shapes=[pltpu.VMEM((128, MOBA_HD), F32),
                        pltpu.VMEM((MOBA_GROUP * MOBA_BLOCK, MOBA_HD), F32)],
        compiler_params=_cp("parallel", "parallel", "arbitrary"),
        name="moba_prompt",
    )(qn, kn, proj)


SAMPLE_BLOCKS_PER_STEP = 4
_PAGES_PER_STEP = SAMPLE_BLOCKS_PER_STEP * (MOBA_BLOCK // PAGE)


def _kmean_kernel(pt_ref, *refs):
    pages = refs[:_PAGES_PER_STEP]
    o_ref = refs[_PAGES_PER_STEP]
    prow = PAGE * MOBA_KV
    r = lax.broadcasted_iota(jnp.int32, (8, prow), 1)
    h = lax.broadcasted_iota(jnp.int32, (8, prow), 0)
    pick = ((r & (MOBA_KV - 1)) == h).astype(F32)
    for i in range(SAMPLE_BLOCKS_PER_STEP):
        s = jnp.dot(pick, pages[2 * i][...], precision=HI, preferred_element_type=F32)
        s = s + jnp.dot(pick, pages[2 * i + 1][...], precision=HI, preferred_element_type=F32)
        o_ref[MOBA_KV * i:MOBA_KV * (i + 1), :] = s[0:MOBA_KV, :] * (1.0 / MOBA_BLOCK)


def _moba_sample_kernel(pt_ref, q_ref, kn_ref, vn_ref, km_ref, *refs):
    kp = refs[:_PAGES_PER_STEP]
    vp = refs[_PAGES_PER_STEP:2 * _PAGES_PER_STEP]
    o_ref, qa_ref, pad_ref, sel_ref, m_ref, l_ref, acc_ref = refs[2 * _PAGES_PER_STEP:]
    n = pl.program_id(1)
    nsteps = pl.num_programs(1)
    nq = q_ref.shape[0]
    rows = MOBA_HEADS * nq
    prow = PAGE * MOBA_KV
    lane = lax.broadcasted_iota(jnp.int32, (rows, 128), 1)
    rowi = lax.broadcasted_iota(jnp.int32, (rows, 128), 0)
    kv_shift = (MOBA_GROUP * nq).bit_length() - 1
    nq_shift = nq.bit_length() - 1
    row_kv = rowi >> kv_shift

    @pl.when(n == 0)
    def _():
        for h in range(MOBA_HEADS):
            qa_ref[h * nq:(h + 1) * nq, :] = q_ref[:, h * MOBA_HD:(h + 1) * MOBA_HD]
        qa = qa_ref[...]
        scores = lax.dot_general(qa, km_ref[...], _NT, precision=HI, preferred_element_type=F32)
        sel_ref[...] = _topk_mask(scores, (lane & (MOBA_KV - 1)) == row_kv, lane, MOBA_TOPK)
        pad_ref[...] = jnp.zeros_like(pad_ref)
        for h in range(MOBA_KV):
            pad_ref[h * nq:(h + 1) * nq, :] = kn_ref[:, h * MOBA_HD:(h + 1) * MOBA_HD]
        logit = lax.dot_general(qa.astype(BF16), pad_ref[...].astype(BF16), _NT, preferred_element_type=F32)
        logit = jnp.where((lane & (nq - 1)) <= (rowi & (nq - 1)), logit, NEG)
        logit = jnp.where((lane >> nq_shift) == row_kv, logit, NEG)
        m0 = jnp.max(logit, axis=-1, keepdims=True)
        p0 = jnp.exp(logit - m0)
        m_ref[...] = m0
        l_ref[...] = jnp.sum(p0, axis=-1, keepdims=True)
        for h in range(MOBA_KV):
            pad_ref[h * nq:(h + 1) * nq, :] = vn_ref[:, h * MOBA_HD:(h + 1) * MOBA_HD]
        acc_ref[...] = jnp.dot(p0.astype(BF16), pad_ref[...].astype(BF16), preferred_element_type=F32)

    qa16 = qa_ref[...].astype(BF16)
    sel = sel_ref[...]
    col = lax.broadcasted_iota(jnp.int32, (rows, prow), 1)
    same_kv = (col & (MOBA_KV - 1)) == (lax.broadcasted_iota(jnp.int32, (rows, prow), 0) >> kv_shift)
    m = m_ref[...]
    l = l_ref[...]
    acc = acc_ref[...]
    for i in range(SAMPLE_BLOCKS_PER_STEP):
        blk = n * SAMPLE_BLOCKS_PER_STEP + i
        gate = jnp.max(jnp.where(lane == blk * MOBA_KV + row_kv, sel, 0.0), axis=-1, keepdims=True) > 0.5
        for half in range(2):
            kpg = kp[2 * i + half][...].astype(BF16)
            vpg = vp[2 * i + half][...].astype(BF16)
            lg = lax.dot_general(qa16, kpg, _NT, preferred_element_type=F32)
            lg = jnp.where(gate, jnp.where(same_kv, lg, NEG), NEG)
            m_new = jnp.maximum(m, jnp.max(lg, axis=-1, keepdims=True))
            alpha = jnp.exp(m - m_new)
            p = jnp.exp(lg - m_new)
            l = alpha * l + jnp.sum(p, axis=-1, keepdims=True)
            acc = alpha * acc + jnp.dot(p.astype(BF16), vpg, preferred_element_type=F32)
            m = m_new
    m_ref[...] = m
    l_ref[...] = l
    acc_ref[...] = acc

    @pl.when(n == nsteps - 1)
    def _():
        out = acc / l
        for h in range(MOBA_HEADS):
            o_ref[:, h * MOBA_HD:(h + 1) * MOBA_HD] = out[h * nq:(h + 1) * nq, :]


def _moba_sample(qn, kn, proj, cache_k, cache_v, page_table, row0, nseq, nq):
    kvw = MOBA_KV * MOBA_HD
    prow = PAGE * MOBA_KV
    n_pages = page_table.shape[1]
    nblk = n_pages * PAGE // MOBA_BLOCK
    nsteps = nblk // SAMPLE_BLOCKS_PER_STEP
    assert nblk * MOBA_KV == 128 and MOBA_BLOCK == 2 * PAGE and MOBA_KV * nq <= 128
    assert nq & (nq - 1) == 0, "new tokens per sequence must be a power of two"
    ck = cache_k.reshape(-1, prow, MOBA_HD)
    cv = cache_v.reshape(-1, prow, MOBA_HD)

    def page(k):
        return pl.BlockSpec((None, prow, MOBA_HD), lambda b, n, pt: (pt[b, n * _PAGES_PER_STEP + k], 0, 0))

    pages = [page(k) for k in range(_PAGES_PER_STEP)]
    kmean = pl.pallas_call(
        _kmean_kernel,
        grid_spec=pltpu.PrefetchScalarGridSpec(
            num_scalar_prefetch=1, grid=(nseq, nsteps),
            in_specs=pages,
            out_specs=pl.BlockSpec((SAMPLE_BLOCKS_PER_STEP * MOBA_KV, MOBA_HD),
                                   lambda b, n, pt: (b * nsteps + n, 0))),
        out_shape=jax.ShapeDtypeStruct((nseq * nblk * MOBA_KV, MOBA_HD), F32),
        compiler_params=_cp("parallel", "parallel"),
        name="moba_kmean",
    )(page_table, *([ck] * _PAGES_PER_STEP))
    rb = row0 // nq
    rows = MOBA_HEADS * nq
    return pl.pallas_call(
        _moba_sample_kernel,
        grid_spec=pltpu.PrefetchScalarGridSpec(
            num_scalar_prefetch=1, grid=(nseq, nsteps),
            in_specs=[pl.BlockSpec((nq, MOBA_HEADS * MOBA_HD), lambda b, n, pt: (rb + b, 0)),
                      pl.BlockSpec((nq, kvw), lambda b, n, pt: (rb + b, 0)),
                      pl.BlockSpec((nq, kvw), lambda b, n, pt: (rb + b, C_V // kvw)),
                      pl.BlockSpec((nblk * MOBA_KV, MOBA_HD), lambda b, n, pt: (b, 0))] + pages + pages,
            out_specs=pl.BlockSpec((nq, MOBA_HEADS * MOBA_HD), lambda b, n, pt: (b, 0)),
            scratch_shapes=[pltpu.VMEM((rows, MOBA_HD), F32),
                            pltpu.VMEM((128, MOBA_HD), F32),
                            pltpu.VMEM((rows, 128), F32),
                            pltpu.VMEM((rows, 1), F32),
                            pltpu.VMEM((rows, 1), F32),
                            pltpu.VMEM((rows, MOBA_HD), F32)]),
        out_shape=jax.ShapeDtypeStruct((nseq * nq, MOBA_HEADS * MOBA_HD), F32),
        compiler_params=_cp("parallel", "arbitrary"),
        name="moba_sample",
    )(page_table, qn, kn, proj, kmean, *([ck] * _PAGES_PER_STEP), *([cv] * _PAGES_PER_STEP))


def _split3(x):
    hi = x.astype(BF16)
    r1 = x - hi.astype(F32)
    mid = r1.astype(BF16)
    lo = (r1 - mid.astype(F32)).astype(BF16)
    return hi, mid, lo


def _dot_exact_rhs(sel, x):
    n = x.shape[1]
    y = jnp.dot(sel, jnp.concatenate(_split3(x), axis=1), preferred_element_type=F32)
    return (y[:, 0:n] + y[:, n:2 * n]) + y[:, 2 * n:3 * n]


def _dot_exact_lhs(x, sel):
    m = x.shape[0]
    y = jnp.dot(jnp.concatenate(_split3(x), axis=0), sel, preferred_element_type=F32)
    return (y[0:m] + y[m:2 * m]) + y[2 * m:3 * m]


def _ssd_kernel(*refs, q, qin, nc, has_init):
    if has_init:
        (z_ref, x_ref, b_ref, c_ref, dt_ref, wx_ref, wb_ref, wc_ref, bx_ref, bb_ref, bc_ref, par_ref, gn_ref,
         s0_ref, cx_ref, cb0_ref, cc0_ref, y_ref, so_ref, ext_ref, st_ref, tr_ref) = refs
    else:
        (z_ref, x_ref, b_ref, c_ref, dt_ref, wx_ref, wb_ref, wc_ref, bx_ref, bb_ref, bc_ref, par_ref, gn_ref,
         y_ref, so_ref, ext_ref, st_ref, tr_ref) = refs
    g = pl.program_id(1)
    c = pl.program_id(2)
    gw = SSD_GW
    ns = SSD_STATE
    cw = gw + 2 * ns

    @pl.when(c == 0)
    def _():
        if has_init:
            st_ref[...] = s0_ref[...]
            ext_ref[0:8, 0:gw] = cx_ref[...]
            ext_ref[0:8, gw:gw + ns] = cb0_ref[...]
            ext_ref[0:8, gw + ns:cw] = cc0_ref[...]
        else:
            st_ref[...] = jnp.zeros_like(st_ref)
            ext_ref[0:8, :] = jnp.zeros((8, cw), F32)
        if qin < q:
            ext_ref[8 + qin:8 + q, :] = jnp.zeros((q - qin, cw), F32)

    ext_ref[8:8 + qin, 0:gw] = x_ref[...]
    ext_ref[8:8 + qin, gw:gw + ns] = b_ref[...]
    ext_ref[8:8 + qin, gw + ns:cw] = c_ref[...]
    w = jnp.concatenate([wx_ref[...], wb_ref[...], wc_ref[...]], axis=1)
    bias = jnp.concatenate([bx_ref[...], bb_ref[...], bc_ref[...]], axis=1)
    conv = bias + ext_ref[5:5 + q, :] * w[0:1]
    conv = conv + ext_ref[6:6 + q, :] * w[1:2]
    conv = conv + ext_ref[7:7 + q, :] * w[2:3]
    conv = conv + ext_ref[8:8 + q, :] * w[3:4]
    if nc > 1:
        ext_ref[0:8, :] = ext_ref[qin:qin + 8, :]
    xbc = conv * jax.nn.sigmoid(conv)
    xs = xbc[:, 0:gw]
    bm = xbc[:, gw:gw + ns]
    cm = xbc[:, gw + ns:cw]

    par = par_ref[...]
    dt = jax.nn.softplus(dt_ref[...] + par[0:1, :])
    if qin < q:
        dt = jnp.concatenate([dt, jnp.zeros((q - qin, 128), F32)], axis=0)
    dta = dt * (-jnp.exp(par[1:2, :]))
    ri = lax.broadcasted_iota(jnp.int32, (q, q), 0)
    ci = lax.broadcasted_iota(jnp.int32, (q, q), 1)
    causal = ri >= ci
    acs = _dot_exact_rhs(causal.astype(BF16), dta)
    acs_last = acs[q - 1:q, :]
    to_end = jnp.exp(acs_last - acs) * dt
    eacs = jnp.exp(acs)
    er = lax.broadcasted_iota(jnp.int32, (128, gw), 0)
    ec = lax.broadcasted_iota(jnp.int32, (128, gw), 1)
    expand = (er == g * 8 + (ec >> 6)).astype(BF16)
    expanded = _dot_exact_lhs(
        jnp.concatenate([to_end, eacs, jnp.broadcast_to(par[2:3, :], (16, 128))], axis=0), expand)
    to_end_x = expanded[0:q]
    eacs_x = expanded[q:2 * q]
    dskip_x = expanded[2 * q:2 * q + 1]
    tr_ref[0] = acs.T
    tr_ref[1] = dt.T
    g8 = pl.multiple_of(g * 8, 8)
    acs_t = tr_ref[0, pl.ds(g8, 8), :]
    dt_t = tr_ref[1, pl.ds(g8, 8), :]
    lane = lax.broadcasted_iota(jnp.int32, (q, 128), 1)
    low = lane < SSD_HD

    cb = lax.dot_general(cm.astype(BF16), bm.astype(BF16), _NT, preferred_element_type=F32)
    ys = []
    for pr in range(4):
        ms = []
        for r in (2 * pr, 2 * pr + 1):
            col = jnp.sum(jnp.where(lane == g * 8 + r, acs, 0.0), axis=1, keepdims=True)
            seg = col - acs_t[r:r + 1, :]
            dec = jnp.where(causal, jnp.exp(jnp.minimum(seg, 0.0)), 0.0)
            ms.append((cb * dec * dt_t[r:r + 1, :]).astype(BF16))
        lhs = jnp.concatenate(ms, axis=1)
        xp = xs[:, pr * 128:(pr + 1) * 128]
        rhs = jnp.concatenate([jnp.where(low, xp, 0.0), jnp.where(low, 0.0, xp)], axis=0).astype(BF16)
        ys.append(jnp.dot(lhs, rhs, preferred_element_type=F32))
    y = jnp.concatenate(ys, axis=1)
    st = st_ref[...]
    y = y + lax.dot_general(cm.astype(BF16), st.astype(BF16), _NT, preferred_element_type=F32) * eacs_x
    y = y + dskip_x * xs
    if qin < q:
        y = y[0:qin, :]
    z = z_ref[...]
    y = y * (z * jax.nn.sigmoid(z))
    msq = jnp.mean(y * y, axis=-1, keepdims=True)
    y_ref[...] = ((y * lax.rsqrt(msq + EPS)) * gn_ref[...]).astype(y_ref.dtype)

    xw = (xs * to_end_x).astype(BF16)
    new = lax.dot_general(xw, bm.astype(BF16), _TN, preferred_element_type=F32)
    dr = lax.broadcasted_iota(jnp.int32, (gw, 128), 0)
    dc = lax.broadcasted_iota(jnp.int32, (gw, 128), 1)
    pick = (dc == g * 8 + (dr >> 6)).astype(BF16)
    last_t = jnp.exp(tr_ref[0, :, q - 1:q])
    cdec = _dot_exact_rhs(pick, jnp.broadcast_to(last_t, (128, ns)))
    st_ref[...] = st * cdec + new

    @pl.when(c == nc - 1)
    def _():
        so_ref[...] = st_ref[...]


def _ssd(proj, projdt, row0, nb, nc, q, qin, par, conv_w, conv_b, gn, out_dtype, s0=None, cst=None):
    gw, ns = SSD_GW, SSD_STATE
    rb = row0 // qin
    zb, xb = C_Z // gw, C_XBC // gw
    bb, cb = (C_XBC + D_INNER) // ns, (C_XBC + D_INNER + SSD_GROUPS * ns) // ns
    row = lambda b, g, c: rb + b * nc + c
    in_specs = [
        pl.BlockSpec((qin, gw), lambda b, g, c: (row(b, g, c), zb + g)),
        pl.BlockSpec((qin, gw), lambda b, g, c: (row(b, g, c), xb + g)),
        pl.BlockSpec((qin, ns), lambda b, g, c: (row(b, g, c), bb + g)),
        pl.BlockSpec((qin, ns), lambda b, g, c: (row(b, g, c), cb + g)),
        pl.BlockSpec((qin, 128), lambda b, g, c: (row(b, g, c), 0)),
        pl.BlockSpec((4, gw), lambda b, g, c: (0, g)),
        pl.BlockSpec((4, ns), lambda b, g, c: (0, D_INNER // ns + g)),
        pl.BlockSpec((4, ns), lambda b, g, c: (0, D_INNER // ns + SSD_GROUPS + g)),
        pl.BlockSpec((1, gw), lambda b, g, c: (0, g)),
        pl.BlockSpec((1, ns), lambda b, g, c: (0, D_INNER // ns + g)),
        pl.BlockSpec((1, ns), lambda b, g, c: (0, D_INNER // ns + SSD_GROUPS + g)),
        pl.BlockSpec((8, 128), lambda b, g, c: (0, 0)),
        pl.BlockSpec((1, gw), lambda b, g, c: (0, g)),
    ]
    args = [proj, proj, proj, proj, projdt, conv_w, conv_w, conv_w, conv_b, conv_b, conv_b, par, gn]
    has_init = s0 is not None
    if has_init:
        in_specs += [
            pl.BlockSpec((None, None, gw, ns), lambda b, g, c: (b, g, 0, 0)),
            pl.BlockSpec((None, 8, gw), lambda b, g, c: (b, 0, g)),
            pl.BlockSpec((None, 8, ns), lambda b, g, c: (b, 0, D_INNER // ns + g)),
            pl.BlockSpec((None, 8, ns), lambda b, g, c: (b, 0, D_INNER // ns + SSD_GROUPS + g)),
        ]
        args += [s0, cst, cst, cst]
    return pl.pallas_call(
        functools.partial(_ssd_kernel, q=q, qin=qin, nc=nc, has_init=has_init),
        grid=(nb, SSD_GROUPS, nc),
        in_specs=in_specs,
        out_specs=[pl.BlockSpec((qin, gw), lambda b, g, c: (b * nc + c, g)),
                   pl.BlockSpec((None, None, gw, ns), lambda b, g, c: (b, g, 0, 0))],
        out_shape=[jax.ShapeDtypeStruct((nb * nc * qin, D_INNER), out_dtype),
                   jax.ShapeDtypeStruct((nb, SSD_GROUPS, gw, ns), F32)],
        scratch_shapes=[pltpu.VMEM((q + 8, gw + 2 * ns), F32),
                        pltpu.VMEM((gw, ns), F32),
                        pltpu.VMEM((2, 128, q), F32)],
        compiler_params=_cp("parallel", "parallel", "arbitrary"),
        name="ssd",
    )(*args)


def _mem_attn_kernel(q_ref, k_ref, v_ref, o_ref):
    s = lax.dot_general(q_ref[...].astype(BF16), k_ref[...].astype(BF16), _NT, preferred_element_type=F32)
    m = jnp.max(s, axis=-1, keepdims=True)
    p = jnp.exp(s - m)
    l = jnp.sum(p, axis=-1, keepdims=True)
    o = jnp.dot(p.astype(BF16), v_ref[...].astype(BF16), preferred_element_type=F32) / l
    o_ref[...] = o.astype(o_ref.dtype)


def _mem_attn(qmn, row0, nb, seq, tq, k2d, k_cb, v2d, v_cb, out_dtype):
    nt = seq // tq
    rb = row0 // tq
    return pl.pallas_call(
        _mem_attn_kernel,
        grid=(nb, nt, MEM_HEADS),
        in_specs=[pl.BlockSpec((tq, MEM_HD), lambda b, i, h: (rb + b * nt + i, h)),
                  pl.BlockSpec((MEM_TOKENS, MEM_HD), lambda b, i, h: (b, k_cb + h)),
                  pl.BlockSpec((MEM_TOKENS, MEM_HD), lambda b, i, h: (b, v_cb + h))],
        out_specs=pl.BlockSpec((tq, MEM_HD), lambda b, i, h: (b * nt + i, h)),
        out_shape=jax.ShapeDtypeStruct((nb * seq, MEM_HEADS * MEM_HD), out_dtype),
        compiler_params=_cp("parallel", "parallel", "parallel"),
        name="mem_attn",
    )(qmn, k2d, v2d)


def _merge_kernel(a_ref, y_ref, m_ref, wa_ref, wb_ref, wc_ref, g0_ref, g1_ref, g2_ref,
                  b0_ref, b1_ref, b2_ref, o_ref):
    oa = jnp.dot(a_ref[...], wa_ref[...], preferred_element_type=F32)
    ob = jnp.dot(y_ref[...], wb_ref[...], preferred_element_type=F32)
    oc = jnp.dot(m_ref[...], wc_ref[...], preferred_element_type=F32)
    o = jax.nn.sigmoid(g0_ref[...] + b0_ref[...]) * oa
    o = o + jax.nn.sigmoid(g1_ref[...] + b1_ref[...]) * ob
    o = o + jax.nn.sigmoid(g2_ref[...] + b2_ref[...]) * oc
    o_ref[...] = o.astype(o_ref.dtype)


def _merge(attn, yssd, memo, wa, wb, wc, projb, bgate, tm=528, tn=512):
    m = attn.shape[0]
    nj = D_MODEL // tn
    gcb = CB_GATE // tn
    act = lambda k: pl.BlockSpec((tm, k), lambda i, j: (i, 0))
    wsp = lambda k: pl.BlockSpec((k, tn), lambda i, j: (0, j))
    gsp = lambda br: pl.BlockSpec((tm, tn), lambda i, j: (i, gcb + br * nj + j))
    bsp = lambda br: pl.BlockSpec((1, tn), lambda i, j: (0, br * nj + j))
    return pl.pallas_call(
        _merge_kernel,
        grid=(m // tm, nj),
        in_specs=[act(attn.shape[1]), act(yssd.shape[1]), act(memo.shape[1]),
                  wsp(wa.shape[0]), wsp(wb.shape[0]), wsp(wc.shape[0]),
                  gsp(0), gsp(1), gsp(2), bsp(0), bsp(1), bsp(2)],
        out_specs=pl.BlockSpec((tm, tn), lambda i, j: (i, j)),
        out_shape=jax.ShapeDtypeStruct((m, D_MODEL), BF16),
        compiler_params=_cp("parallel", "parallel"),
        name="merge",
    )(attn, yssd, memo, wa, wb, wc, projb, projb, projb, bgate, bgate, bgate)


def _router_kernel(x_ref, g_ref, w_ref, b_ref, h_ref, idx_ref, p_ref):
    x = x_ref[...]
    ms = jnp.mean(x * x, axis=-1, keepdims=True)
    h = (x * lax.rsqrt(ms + EPS)) * g_ref[...]
    h_ref[...] = h
    logits = jnp.dot(h, w_ref[...], precision=HI, preferred_element_type=F32) + b_ref[...]
    lane = lax.broadcasted_iota(jnp.int32, logits.shape, 1)
    lanef = lane.astype(F32)
    s = jnp.where(lane < N_EXPERTS, logits, -jnp.inf)
    vals, idxs = [], []
    for _ in range(TOP_K):
        m = jnp.max(s, axis=-1, keepdims=True)
        idx = jnp.min(jnp.where(s == m, lanef, 128.0), axis=-1, keepdims=True)
        vals.append(m)
        idxs.append(idx)
        s = jnp.where(lanef == idx, -jnp.inf, s)
    es = [jnp.exp(v - vals[0]) for v in vals]
    den = es[0] + es[1] + es[2] + es[3]
    p_out = jnp.zeros(logits.shape, F32)
    i_out = jnp.zeros(logits.shape, F32)
    for k in range(TOP_K):
        p_out = jnp.where(lane == k, es[k] / den, p_out)
        i_out = jnp.where(lane == k, idxs[k], i_out)
    p_ref[...] = p_out
    idx_ref[...] = i_out.astype(jnp.int32)


def _router(x2, g, w_router, b_router, tr=264):
    m, d = x2.shape
    wr = jnp.zeros((d, 128), F32).at[:, :N_EXPERTS].set(w_router)
    br = jnp.zeros((1, 128), F32).at[0, :N_EXPERTS].set(b_router)
    return pl.pallas_call(
        _router_kernel,
        grid=(m // tr,),
        in_specs=[pl.BlockSpec((tr, d), lambda i: (i, 0)),
                  pl.BlockSpec((1, d), lambda i: (0, 0)),
                  pl.BlockSpec((d, 128), lambda i: (0, 0)),
                  pl.BlockSpec((1, 128), lambda i: (0, 0))],
        out_specs=[pl.BlockSpec((tr, d), lambda i: (i, 0)),
                   pl.BlockSpec((tr, 128), lambda i: (i, 0)),
                   pl.BlockSpec((tr, 128), lambda i: (i, 0))],
        out_shape=[jax.ShapeDtypeStruct((m, d), F32),
                   jax.ShapeDtypeStruct((m, 128), jnp.int32),
                   jax.ShapeDtypeStruct((m, 128), F32)],
        compiler_params=_cp("parallel"),
        name="router",
    )(x2, g.reshape(1, d), wr, br)


MOE_TM = 256
MOE_TN_UP = 512
MOE_TN_DOWN = 1024
COMBINE_TOKENS = 64
DMA_UNROLL = 8


def _lookup(table, idx):
    experts = jnp.arange(N_EXPERTS, dtype=jnp.int32)
    return jnp.sum(jnp.where(idx[:, None] == experts[None, :], table[None, :], 0), axis=1, dtype=jnp.int32)


def _dispatch(ridx, tm, n_tiles):
    na = ridx.shape[0] * TOP_K
    e_flat = ridx.reshape(na)
    experts = jnp.arange(N_EXPERTS, dtype=jnp.int32)
    counts = jnp.sum(e_flat[:, None] == experts[None, :], axis=0, dtype=jnp.int32)
    nt = (counts + tm - 1) // tm
    t_end = jnp.cumsum(nt)
    t0 = t_end - nt
    n_used = t_end[-1]
    start = jnp.cumsum(counts) - counts
    order = jnp.argsort(e_flat, stable=True).astype(jnp.int32)
    inv = jnp.argsort(order).astype(jnp.int32)
    pos = _lookup(t0 * tm - start, e_flat) + inv
    tiles = jnp.arange(n_tiles, dtype=jnp.int32)
    tile_e = jnp.minimum(jnp.sum(t_end[None, :] <= tiles[:, None], axis=1, dtype=jnp.int32), N_EXPERTS - 1)
    tile_t0 = _lookup(t0, tile_e)
    tile_nt = _lookup(nt, tile_e)
    k = tiles - tile_t0
    used = tiles < n_used
    tile_base = jnp.where(used, _lookup(start, tile_e) + k * tm, 0)
    tile_nv = jnp.where(used, jnp.clip(_lookup(counts, tile_e) - k * tm, 0, tm), 0)
    return order, tile_base, tile_nv, pos, (tile_e, tile_nt, tile_t0, n_used)


def _schedule(tables, n_tiles, nj):
    tile_e, tile_nt, tile_t0, n_used = tables
    s = jnp.arange(n_tiles * nj, dtype=jnp.int32)
    last = nj * n_used - 1
    valid = s <= last
    rep = lambda v: jnp.broadcast_to(v[:, None], (n_tiles, nj)).reshape(n_tiles * nj)
    e, nt, t0 = rep(tile_e), jnp.maximum(rep(tile_nt), 1), rep(tile_t0)
    local = s - nj * t0
    sj = local // nt
    si = t0 + local % nt
    first = valid & (local % nt == 0)
    nxt = jnp.minimum(s + nt, last)
    has_next = first & (s + nt <= last)
    keep = lambda v: jnp.where(valid, v, v[last])
    i32 = lambda v: v.astype(jnp.int32)
    return (keep(e), keep(sj), keep(si), i32(first), i32(valid),
            jnp.where(valid, si, s // nj), jnp.where(valid, sj, s % nj),
            e[nxt], sj[nxt], i32(has_next))


def _row_gather_pipeline(n_rows, issue_row, wait_row, step, n_steps):
    def each_row(count, fn):
        full = count // DMA_UNROLL

        def body(gi, carry):
            for u in range(DMA_UNROLL):
                fn(gi * DMA_UNROLL + u)
            return carry

        def tail(r, carry):
            fn(r)
            return carry

        lax.fori_loop(0, full, body, 0)
        lax.fori_loop(full * DMA_UNROLL, count, tail, 0)

    @pl.when(step == 0)
    def _():
        each_row(n_rows(0), lambda r: issue_row(0, 0, r))

    @pl.when(step + 1 < n_steps)
    def _():
        each_row(n_rows(step + 1), lambda r: issue_row(step + 1, (step + 1) & 1, r))

    each_row(n_rows(step), lambda r: wait_row(step & 1, r))


def _gather_kernel(order_ref, base_ref, nv_ref, h_hbm, o_ref, buf_ref, sem):
    i = pl.program_id(0)

    @pl.when(i == 0)
    def _():
        buf_ref[...] = jnp.zeros_like(buf_ref)

    def copy(slot, r, src_row):
        return pltpu.make_async_copy(h_hbm.at[pl.ds(src_row, 1), :], buf_ref.at[slot, pl.ds(r, 1), :],
                                     sem.at[slot])

    _row_gather_pipeline(
        lambda st: nv_ref[st],
        lambda st, slot, r: copy(slot, r, order_ref[base_ref[st] + r] // TOP_K).start(),
        lambda slot, r: copy(slot, r, 0).wait(),
        i, pl.num_programs(0))
    o_ref[...] = buf_ref[i & 1].astype(o_ref.dtype)


def _gather_rows(order, tile_base, tile_nv, h2, rows):
    n_tiles = tile_base.shape[0]
    d = h2.shape[1]
    return pl.pallas_call(
        _gather_kernel,
        grid_spec=pltpu.PrefetchScalarGridSpec(
            num_scalar_prefetch=3, grid=(n_tiles,),
            in_specs=[pl.BlockSpec(memory_space=pl.ANY)],
            out_specs=pl.BlockSpec((rows, d), lambda i, *_: (i, 0)),
            scratch_shapes=[pltpu.VMEM((2, rows, d), F32), pltpu.SemaphoreType.DMA((2,))]),
        out_shape=jax.ShapeDtypeStruct((n_tiles * rows, d), BF16),
        compiler_params=_cp("arbitrary"),
        name="moe_gather",
    )(order, tile_base, tile_nv, h2)


def _weight_block_copies(w_hbms, bufs, sem, e, j):
    tn = bufs[0].shape[1]
    c0 = pl.multiple_of(j * tn, tn)
    return [pltpu.make_async_copy(w.at[e, :, pl.ds(c0, tn)], b, sem.at[k])
            for k, (w, b) in enumerate(zip(w_hbms, bufs))]


def _stage_weights(s, se, sj, ne, nj, has_next, w_hbms, bufs, w16s, sem):
    @pl.when(s == 0)
    def _():
        for c in _weight_block_copies(w_hbms, bufs, sem, se[0], sj[0]):
            c.start()

    for c in _weight_block_copies(w_hbms, bufs, sem, se[s], sj[s]):
        c.wait()
    for b, w16 in zip(bufs, w16s):
        w16[...] = b[...].astype(BF16)

    @pl.when(has_next[s] == 1)
    def _():
        for c in _weight_block_copies(w_hbms, bufs, sem, ne[s], nj[s]):
            c.start()


def _moe_up_kernel(se, sj, si, sfirst, svalid, so_i, so_j, ne, nj, has_next,
                   x_ref, wg_hbm, wu_hbm, bg_ref, bu_ref, o_ref, wgf, wuf, wg16, wu16, sem):
    s = pl.program_id(0)

    @pl.when(svalid[s] == 1)
    def _():
        @pl.when(sfirst[s] == 1)
        def _():
            _stage_weights(s, se, sj, ne, nj, has_next, (wg_hbm, wu_hbm), (wgf, wuf), (wg16, wu16), sem)

        x = x_ref[...]
        g = jnp.minimum(jnp.dot(x, wg16[...], preferred_element_type=F32) + bg_ref[...], SWIGLU_LIMIT)
        u = jnp.clip(jnp.dot(x, wu16[...], preferred_element_type=F32) + bu_ref[...], -SWIGLU_LIMIT, SWIGLU_LIMIT)
        act = (u + 1.0) * (g * jax.nn.sigmoid(SWIGLU_ALPHA * g))
        o_ref[...] = act.astype(o_ref.dtype)

    @pl.when(svalid[s] == 0)
    def _():
        o_ref[...] = jnp.zeros_like(o_ref)


def _moe_down_kernel(se, sj, si, sfirst, svalid, so_i, so_j, ne, nj, has_next,
                     a_ref, w_hbm, b_ref, o_ref, wf, w16, sem):
    s = pl.program_id(0)

    @pl.when(svalid[s] == 1)
    def _():
        @pl.when(sfirst[s] == 1)
        def _():
            _stage_weights(s, se, sj, ne, nj, has_next, (w_hbm,), (wf,), (w16,), sem)

        o_ref[...] = jnp.dot(a_ref[...], w16[...], preferred_element_type=F32) + b_ref[...]

    @pl.when(svalid[s] == 0)
    def _():
        o_ref[...] = jnp.zeros_like(o_ref)


def _moe_up(sched, xs, wg, wu, bg, bu, tm, tn):
    rows, d = xs.shape
    f = wg.shape[2]
    nsteps = sched[0].shape[0]
    hbm = pl.BlockSpec(memory_space=pl.ANY)
    bsp = pl.BlockSpec((None, 1, tn), lambda s, se, sj, *_: (se[s], 0, sj[s]))
    return pl.pallas_call(
        _moe_up_kernel,
        grid_spec=pltpu.PrefetchScalarGridSpec(
            num_scalar_prefetch=len(sched), grid=(nsteps,),
            in_specs=[pl.BlockSpec((tm, d), lambda s, se, sj, si, *_: (si[s], 0)), hbm, hbm, bsp, bsp],
            out_specs=pl.BlockSpec((tm, tn), lambda s, se, sj, si, sf, sv, oi, oj, *_: (oi[s], oj[s])),
            scratch_shapes=[pltpu.VMEM((d, tn), F32), pltpu.VMEM((d, tn), F32),
                            pltpu.VMEM((d, tn), BF16), pltpu.VMEM((d, tn), BF16),
                            pltpu.SemaphoreType.DMA((2,))]),
        out_shape=jax.ShapeDtypeStruct((rows, f), BF16),
        compiler_params=_cp("arbitrary"),
        name="moe_up",
    )(*sched, xs, wg, wu, bg, bu)


def _moe_down(sched, act, wd, bd, tm, tn):
    rows, f = act.shape
    d = wd.shape[2]
    nsteps = sched[0].shape[0]
    return pl.pallas_call(
        _moe_down_kernel,
        grid_spec=pltpu.PrefetchScalarGridSpec(
            num_scalar_prefetch=len(sched), grid=(nsteps,),
            in_specs=[pl.BlockSpec((tm, f), lambda s, se, sj, si, *_: (si[s], 0)),
                      pl.BlockSpec(memory_space=pl.ANY),
                      pl.BlockSpec((None, 1, tn), lambda s, se, sj, *_: (se[s], 0, sj[s]))],
            out_specs=pl.BlockSpec((tm, tn), lambda s, se, sj, si, sf, sv, oi, oj, *_: (oi[s], oj[s])),
            scratch_shapes=[pltpu.VMEM((f, tn), F32), pltpu.VMEM((f, tn), BF16),
                            pltpu.SemaphoreType.DMA((1,))]),
        out_shape=jax.ShapeDtypeStruct((rows, d), F32),
        compiler_params=_cp("arbitrary"),
        name="moe_down",
    )(*sched, act, wd, bd)


def _combine_kernel(pos_ref, x_ref, p_ref, y_hbm, o_ref, buf_ref, sem):
    i = pl.program_id(0)
    n = COMBINE_TOKENS * TOP_K

    def copy(slot, r, src_row):
        return pltpu.make_async_copy(y_hbm.at[pl.ds(src_row, 1), :],
                                     buf_ref.at[slot, r & (TOP_K - 1), pl.ds(r >> 2, 1), :], sem.at[slot])

    _row_gather_pipeline(
        lambda st: n,
        lambda st, slot, r: copy(slot, r, pos_ref[st * n + r]).start(),
        lambda slot, r: copy(slot, r, 0).wait(),
        i, pl.num_programs(0))
    slot = i & 1
    p = p_ref[...]
    acc = x_ref[...]
    for k in range(TOP_K):
        acc = acc + p[:, k:k + 1] * buf_ref[slot, k]
    o_ref[...] = acc


def _combine(pos, x2, rprob, ysort):
    t, d = x2.shape
    assert TOP_K == 4
    return pl.pallas_call(
        _combine_kernel,
        grid_spec=pltpu.PrefetchScalarGridSpec(
            num_scalar_prefetch=1, grid=(t // COMBINE_TOKENS,),
            in_specs=[pl.BlockSpec((COMBINE_TOKENS, d), lambda i, pos: (i, 0)),
                      pl.BlockSpec((COMBINE_TOKENS, 128), lambda i, pos: (i, 0)),
                      pl.BlockSpec(memory_space=pl.ANY)],
            out_specs=pl.BlockSpec((COMBINE_TOKENS, d), lambda i, pos: (i, 0)),
            scratch_shapes=[pltpu.VMEM((2, TOP_K, COMBINE_TOKENS, d), F32), pltpu.SemaphoreType.DMA((2,))]),
        out_shape=jax.ShapeDtypeStruct((t, d), F32),
        compiler_params=_cp("arbitrary"),
        name="moe_combine",
    )(pos, x2, rprob, ysort)


def _moe(x2, norm_g, w_router, b_router, wg, bg, wu, bu, wd, bd):
    t = x2.shape[0]
    h2, ridx, rprob = _router(x2, norm_g, w_router, b_router)
    n_tiles = (t * TOP_K) // MOE_TM + N_EXPERTS
    order, tile_base, tile_nv, pos, tables = _dispatch(ridx[:, :TOP_K], MOE_TM, n_tiles)
    xs = _gather_rows(order, tile_base, tile_nv, h2, MOE_TM)
    act = _moe_up(_schedule(tables, n_tiles, D_FF // MOE_TN_UP), xs, wg, wu,
                  bg.reshape(N_EXPERTS, 1, D_FF), bu.reshape(N_EXPERTS, 1, D_FF), MOE_TM, MOE_TN_UP)
    ysort = _moe_down(_schedule(tables, n_tiles, D_MODEL // MOE_TN_DOWN), act, wd,
                      bd.reshape(N_EXPERTS, 1, D_MODEL), MOE_TM, MOE_TN_DOWN)
    return _combine(pos, x2, rprob, ysort)


def kernel(x_prompt, x_sample, mem_prompt, cache_k, cache_v, page_table, state_ssm, state_conv, cache_mem_k, cache_mem_v, norm_mix_g, w_in, b_branch_gate, moba_q_norm_g, moba_k_norm_g, conv_w, conv_b, dt_bias, a_log, d_skip, ssd_norm_g, mem_norm_g, w_mem_kv, mem_q_norm_g, mem_k_norm_g, w_br_moba, w_br_ssd, w_br_mem, w_out, norm_ffn_g, w_router, b_router, w_exp_gate, b_exp_gate, w_exp_up, b_exp_up, w_exp_down, b_exp_down):
    assert w_in.shape[0] == 1, "single layer"
    nb, seq, d = x_prompt.shape
    ns, nq, _ = x_sample.shape
    tp = nb * seq
    x_all = jnp.concatenate([x_prompt.reshape(tp, d), x_sample.reshape(ns * nq, d)], axis=0)

    w_in2 = w_in.reshape(d, -1)
    h = _rmsnorm(x_all, norm_mix_g[0], BF16)
    w_in_t = w_in2.T
    n_in = w_in_t.shape[0]
    proja = _in_proj(h, w_in_t, 0, W_IN_A, tm=1056, tn=512, name="in_proj_a")
    projdt = _in_proj(h, w_in_t, W_IN_A, W_IN_B - W_IN_A, tm=1056, tn=128, name="in_proj_dt")
    projb = _in_proj(h, w_in_t, W_IN_B, n_in - W_IN_B, tm=1056, tn=512, name="in_proj_b")

    qn = _head_norm(proja, C_Q, MOBA_HEADS * MOBA_HD, MOBA_HD, moba_q_norm_g[0], MOBA_HD ** -0.5)
    kn = _head_norm(proja, C_K, MOBA_KV * MOBA_HD, MOBA_HD, moba_k_norm_g[0], 1.0)
    qmn = _head_norm(projb, CB_QM, MEM_HEADS * MEM_HD, MEM_HD, mem_q_norm_g[0], MEM_HD ** -0.5)

    attn_p = _moba_prompt(qn, kn, proja, nb, seq)
    attn_s = _moba_sample(qn, kn, proja, cache_k, cache_v, page_table, tp, ns, nq)
    attn = jnp.concatenate([attn_p, attn_s.astype(BF16)], axis=0)

    par = jnp.zeros((8, 128), F32)
    par = par.at[0, :SSD_HEADS].set(dt_bias[0]).at[1, :SSD_HEADS].set(a_log[0]).at[2, :SSD_HEADS].set(d_skip[0])
    gn = ssd_norm_g.reshape(1, D_INNER)
    cw2 = conv_w.reshape(-1, CONV_CH)
    cb2 = conv_b.reshape(1, CONV_CH)
    y_p, st_p = _ssd(proja, projdt, 0, nb, seq // 256, 256, 256, par, cw2, cb2, gn, BF16)
    s0 = state_ssm.reshape(ns, SSD_GROUPS, SSD_GW, SSD_STATE)
    cst = jnp.concatenate([jnp.zeros((ns, 5, CONV_CH), F32), state_conv.reshape(ns, 3, CONV_CH)], axis=1)
    y_s, st_s = _ssd(proja, projdt, tp, ns, 1, 128, nq, par, cw2, cb2, gn, F32, s0=s0, cst=cst)
    yssd = jnp.concatenate([y_p, y_s.astype(BF16)], axis=0)

    hm = _rmsnorm(mem_prompt.reshape(nb * MEM_TOKENS, d), mem_norm_g[0], BF16)
    memkv = _matmul(hm, w_mem_kv.reshape(d, -1).astype(BF16), tm=512, tn=512, name="mem_kv")
    mw = MEM_HEADS * MEM_HD
    mk = _head_norm(memkv, 0, mw, MEM_HD, mem_k_norm_g[0], 1.0, tr=256)
    memo_p = _mem_attn(qmn, 0, nb, seq, 256, mk, 0, memkv, MEM_HEADS, BF16)
    cmk = cache_mem_k.reshape(ns * MEM_TOKENS, mw)
    cmv = cache_mem_v.reshape(ns * MEM_TOKENS, mw)
    memo_s = _mem_attn(qmn, tp, ns, nq, nq, cmk, 0, cmv, 0, F32)
    memo = jnp.concatenate([memo_p, memo_s.astype(BF16)], axis=0)

    merged = _merge(attn, yssd, memo, w_br_moba.reshape(-1, d).astype(BF16), w_br_ssd.reshape(-1, d).astype(BF16),
                    w_br_mem.reshape(-1, d).astype(BF16), projb, b_branch_gate.reshape(1, -1))
    x2 = _matmul(merged, w_out.reshape(d, d).astype(BF16), tm=1056, tn=512, residual=x_all, name="out_proj")

    y = _moe(x2, norm_ffn_g[0], w_router.reshape(d, N_EXPERTS), b_router[0],
             w_exp_gate.reshape(N_EXPERTS, d, D_FF), b_exp_gate.reshape(N_EXPERTS, D_FF),
             w_exp_up.reshape(N_EXPERTS, d, D_FF), b_exp_up.reshape(N_EXPERTS, D_FF),
             w_exp_down.reshape(N_EXPERTS, D_FF, d), b_exp_down.reshape(N_EXPERTS, d))

    kvw = MOBA_KV * MOBA_HD
    v_all = proja[:, C_V:C_V + kvw]
    conv_p = jnp.stack([proja[(b + 1) * seq - 3:(b + 1) * seq, C_XBC:C_XBC + CONV_CH] for b in range(nb)])
    conv_s = proja[tp:, C_XBC:C_XBC + CONV_CH].reshape(ns, nq, CONV_CH)[:, nq - 3:]
    return (
        y[:tp].reshape(nb, seq, d),
        y[tp:].reshape(ns, nq, d),
        kn[:tp].reshape(1, nb, seq, MOBA_KV, MOBA_HD),
        v_all[:tp].reshape(1, nb, seq, MOBA_KV, MOBA_HD),
        kn[tp:].reshape(1, ns, nq, MOBA_KV, MOBA_HD),
        v_all[tp:].reshape(1, ns, nq, MOBA_KV, MOBA_HD),
        st_p.reshape(1, nb, SSD_HEADS, SSD_HD, SSD_STATE),
        conv_p[None],
        st_s.reshape(1, ns, SSD_HEADS, SSD_HD, SSD_STATE),
        conv_s[None],
        mk.reshape(1, nb, MEM_TOKENS, MEM_HEADS, MEM_HD),
        memkv[:, mw:].reshape(1, nb, MEM_TOKENS, MEM_HEADS, MEM_HD),
    )
```

```python
import functools

import jax
import jax.numpy as jnp
from jax import lax
from jax.experimental import pallas as pl
from jax.experimental.pallas import tpu as pltpu

F32 = jnp.float32
BF16 = jnp.bfloat16
HI = lax.Precision.HIGHEST

D_MODEL = 4096
EPS = 1e-6
NEG = -1e30

MOBA_HEADS = 16
MOBA_KV = 4
MOBA_GROUP = 4
MOBA_HD = 128
MOBA_BLOCK = 256
MOBA_TOPK = 3
PAGE = 128
D_INNER = 4096
SSD_HEADS = 64
SSD_HD = 64
SSD_GROUPS = 8
SSD_STATE = 128
SSD_GW = D_INNER // SSD_GROUPS
CONV_CH = D_INNER + 2 * SSD_GROUPS * SSD_STATE
MEM_TOKENS = 256
MEM_HEADS = 4
MEM_HD = 512
N_EXPERTS = 32
TOP_K = 4
D_FF = 4096
SWIGLU_LIMIT = 7.0
SWIGLU_ALPHA = 1.702

C_Q = 0
C_K = 2048
C_V = 2560
C_Z = 3072
C_XBC = 7168
W_IN_A = 13312
W_IN_B = 13376
CB_QM = 0
CB_GATE = 2048

V7X_VMEM_LIMIT = 56 * 1024 * 1024

_NT = (((1,), (1,)), ((), ()))
_TN = (((0,), (0,)), ((), ()))


def _cp(*sem):
    return pltpu.CompilerParams(dimension_semantics=sem, vmem_limit_bytes=V7X_VMEM_LIMIT)


def _rmsnorm_kernel(x_ref, g_ref, o_ref):
    x = x_ref[...]
    ms = jnp.mean(x * x, axis=-1, keepdims=True)
    o_ref[...] = ((x * lax.rsqrt(ms + EPS)) * g_ref[...]).astype(o_ref.dtype)


def _rmsnorm(x, g, out_dtype, tr=256):
    m, d = x.shape
    return pl.pallas_call(
        _rmsnorm_kernel,
        grid=(m // tr,),
        in_specs=[pl.BlockSpec((tr, d), lambda i: (i, 0)), pl.BlockSpec((1, d), lambda i: (0, 0))],
        out_specs=pl.BlockSpec((tr, d), lambda i: (i, 0)),
        out_shape=jax.ShapeDtypeStruct((m, d), out_dtype),
        compiler_params=_cp("parallel"),
        name="rmsnorm",
    )(x, g.reshape(1, d))


def _head_norm_kernel(x_ref, g_ref, o_ref, *, hd, scale):
    x = x_ref[...]
    g = g_ref[...]
    for s in range(x.shape[1] // hd):
        xs = x[:, s * hd:(s + 1) * hd]
        ms = jnp.mean(xs * xs, axis=-1, keepdims=True)
        y = (xs * lax.rsqrt(ms + EPS)) * g[:, s * hd:(s + 1) * hd]
        if scale != 1.0:
            y = y * scale
        o_ref[:, s * hd:(s + 1) * hd] = y


def _head_norm(src, col_off, width, hd, gain, scale, tr=264):
    rows = src.shape[0]
    cb = col_off // 512
    return pl.pallas_call(
        functools.partial(_head_norm_kernel, hd=hd, scale=scale),
        grid=(rows // tr, width // 512),
        in_specs=[pl.BlockSpec((tr, 512), lambda i, j: (i, cb + j)),
                  pl.BlockSpec((1, 512), lambda i, j: (0, 0))],
        out_specs=pl.BlockSpec((tr, 512), lambda i, j: (i, j)),
        out_shape=jax.ShapeDtypeStruct((rows, width), F32),
        compiler_params=_cp("parallel", "parallel"),
        name="head_norm",
    )(src, jnp.tile(gain, 512 // hd).reshape(1, 512))


def _mm_kernel(a_ref, w_ref, o_ref):
    o_ref[...] = jnp.dot(a_ref[...], w_ref[...], preferred_element_type=F32).astype(o_ref.dtype)


def _mm_res_kernel(a_ref, w_ref, r_ref, o_ref):
    o_ref[...] = r_ref[...] + jnp.dot(a_ref[...], w_ref[...], preferred_element_type=F32)


def _matmul(a, w, tm, tn, out_dtype=F32, residual=None, name="matmul"):
    m, k = a.shape
    n = w.shape[1]
    in_specs = [pl.BlockSpec((tm, k), lambda i, j: (i, 0)), pl.BlockSpec((k, tn), lambda i, j: (0, j))]
    args = [a, w]
    body = _mm_kernel
    if residual is not None:
        in_specs.append(pl.BlockSpec((tm, tn), lambda i, j: (i, j)))
        args.append(residual)
        body = _mm_res_kernel
    return pl.pallas_call(
        body,
        grid=(m // tm, n // tn),
        in_specs=in_specs,
        out_specs=pl.BlockSpec((tm, tn), lambda i, j: (i, j)),
        out_shape=jax.ShapeDtypeStruct((m, n), out_dtype),
        compiler_params=_cp("parallel", "parallel"),
        name=name,
    )(*args)


W_CHUNK = 64


def _in_proj_kernel(a_ref, *refs, nchunks):
    w_refs = refs[:nchunks]
    o_ref, w16_ref = refs[nchunks:]

    @pl.when(pl.program_id(1) == 0)
    def _():
        for k, w in enumerate(w_refs):
            w16_ref[k * W_CHUNK:(k + 1) * W_CHUNK, :] = w[...].astype(BF16)
        if nchunks * W_CHUNK < w16_ref.shape[0]:
            w16_ref[nchunks * W_CHUNK:, :] = jnp.zeros((w16_ref.shape[0] - nchunks * W_CHUNK, w16_ref.shape[1]), BF16)

    o_ref[...] = lax.dot_general(a_ref[...], w16_ref[...], _NT, preferred_element_type=F32)


def _in_proj(a, wt, row0, n, tm, tn, name):
    m, k = a.shape
    nchunks = min(n, tn) // W_CHUNK
    ntiles = max(n // tn, 1)
    assert row0 % W_CHUNK == 0 and (n % tn == 0 or n < tn) and n % W_CHUNK == 0
    wspec = lambda c: pl.BlockSpec((W_CHUNK, k), lambda j, i: (row0 // W_CHUNK + j * nchunks + c, 0))
    return pl.pallas_call(
        functools.partial(_in_proj_kernel, nchunks=nchunks),
        grid=(ntiles, m // tm),
        in_specs=[pl.BlockSpec((tm, k), lambda j, i: (i, 0))] + [wspec(c) for c in range(nchunks)],
        out_specs=pl.BlockSpec((tm, tn), lambda j, i: (i, j)),
        out_shape=jax.ShapeDtypeStruct((m, ntiles * tn), F32),
        scratch_shapes=[pltpu.VMEM((tn, k), BF16)],
        compiler_params=_cp("parallel", "arbitrary"),
        name=name,
    )(a, *([wt] * nchunks))


def _topk_mask(s, valid, lane, k):
    lanef = lane.astype(F32)
    s = jnp.where(valid, s, -jnp.inf)
    sel = jnp.zeros(s.shape, F32)
    for _ in range(k):
        m = jnp.max(s, axis=-1, keepdims=True)
        idx = jnp.min(jnp.where(s == m, lanef, 128.0), axis=-1, keepdims=True)
        hit = lanef == idx
        sel = jnp.where(hit, 1.0, sel)
        s = jnp.where(hit, -jnp.inf, s)
    return jnp.where(valid, sel, 0.0)


def _moba_prompt_kernel(q_ref, k_ref, v_ref, o_ref, kmean_ref, acc_ref):
    qb = pl.program_id(2)
    blk = MOBA_BLOCK
    grp = MOBA_GROUP
    rows = grp * blk
    nblk = k_ref.shape[0] // blk

    @pl.when(qb == 0)
    def _():
        kmean_ref[...] = jnp.zeros_like(kmean_ref)
        for n in range(nblk):
            kmean_ref[n:n + 1, :] = jnp.sum(k_ref[n * blk:(n + 1) * blk, :], axis=0, keepdims=True) * (1.0 / blk)

    q4 = q_ref[...]
    q = jnp.concatenate([q4[:, g * MOBA_HD:(g + 1) * MOBA_HD] for g in range(grp)], axis=0)
    scores = lax.dot_general(q, kmean_ref[...], _NT, precision=HI, preferred_element_type=F32)
    lane = lax.broadcasted_iota(jnp.int32, (rows, 128), 1)
    sel = _topk_mask(scores, lane < qb, lane, MOBA_TOPK)
    q16 = q.astype(BF16)

    own = pl.multiple_of(qb * blk, blk)
    k_own = k_ref[pl.ds(own, blk), :].astype(BF16)
    v_own = v_ref[pl.ds(own, blk), :].astype(BF16)
    logit = lax.dot_general(q16, k_own, _NT, preferred_element_type=F32)
    t_row = lax.broadcasted_iota(jnp.int32, (rows, blk), 0) & (blk - 1)
    j_col = lax.broadcasted_iota(jnp.int32, (rows, blk), 1)
    logit = jnp.where(j_col <= t_row, logit, NEG)
    m0 = jnp.max(logit, axis=-1, keepdims=True)
    p0 = jnp.exp(logit - m0)
    l0 = jnp.sum(p0, axis=-1, keepdims=True)
    acc_ref[...] = jnp.dot(p0.astype(BF16), v_own, preferred_element_type=F32)

    def body(n, carry):
        m, l = carry
        gate = jnp.max(jnp.where(lane == n, sel, 0.0), axis=-1, keepdims=True) > 0.5
        off = pl.multiple_of(n * blk, blk)
        kb = k_ref[pl.ds(off, blk), :].astype(BF16)
        vb = v_ref[pl.ds(off, blk), :].astype(BF16)
        lg = lax.dot_general(q16, kb, _NT, preferred_element_type=F32)
        lg = jnp.where(gate, lg, NEG)
        m_new = jnp.maximum(m, jnp.max(lg, axis=-1, keepdims=True))
        alpha = jnp.exp(m - m_new)
        p = jnp.exp(lg - m_new)
        l = alpha * l + jnp.sum(p, axis=-1, keepdims=True)
        acc_ref[...] = alpha * acc_ref[...] + jnp.dot(p.astype(BF16), vb, preferred_element_type=F32)
        return m_new, l

    _, l = lax.fori_loop(0, qb, body, (m0, l0))
    out = acc_ref[...] / l
    for g in range(grp):
        o_ref[:, g * MOBA_HD:(g + 1) * MOBA_HD] = out[g * blk:(g + 1) * blk, :].astype(o_ref.dtype)


def _moba_prompt(qn, kn, proj, nb, seq):
    nqb = seq // MOBA_BLOCK
    gw = MOBA_GROUP * MOBA_HD
    vcb = C_V // MOBA_HD
    return pl.pallas_call(
        _moba_prompt_kernel,
        grid=(nb, MOBA_KV, nqb),
        in_specs=[pl.BlockSpec((MOBA_BLOCK, gw), lambda b, h, i: (b * nqb + i, h)),
                  pl.BlockSpec((seq, MOBA_HD), lambda b, h, i: (b, h)),
                  pl.BlockSpec((seq, MOBA_HD), lambda b, h, i: (b, vcb + h))],
        out_specs=pl.BlockSpec((MOBA_BLOCK, gw), lambda b, h, i: (b * nqb + i, h)),
        out_shape=jax.ShapeDtypeStruct((nb * seq, MOBA_HEADS * MOBA_HD), BF16),
        scratch_shapes=[pltpu.VMEM((128, MOBA_HD), F32),
                        pltpu.VMEM((MOBA_GROUP * MOBA_BLOCK, MOBA_HD), F32)],
        compiler_params=_cp("parallel", "parallel", "arbitrary"),
        name="moba_prompt",
    )(qn, kn, proj)


SAMPLE_BLOCKS_PER_STEP = 4
_PAGES_PER_STEP = SAMPLE_BLOCKS_PER_STEP * (MOBA_BLOCK // PAGE)


def _kmean_kernel(pt_ref, *refs):
    pages = refs[:_PAGES_PER_STEP]
    o_ref = refs[_PAGES_PER_STEP]
    prow = PAGE * MOBA_KV
    r = lax.broadcasted_iota(jnp.int32, (16, prow), 1)
    h = lax.broadcasted_iota(jnp.int32, (16, prow), 0)
    pick = ((r & (MOBA_KV - 1)) == h).astype(BF16)
    for i in range(SAMPLE_BLOCKS_PER_STEP):
        s = _dot_exact_rhs(pick, pages[2 * i][...]) + _dot_exact_rhs(pick, pages[2 * i + 1][...])
        o_ref[MOBA_KV * i:MOBA_KV * (i + 1), :] = s[0:MOBA_KV, :] * (1.0 / MOBA_BLOCK)


def _moba_sample_kernel(pt_ref, q_ref, kn_ref, vn_ref, km_ref, *refs):
    kp = refs[:_PAGES_PER_STEP]
    vp = refs[_PAGES_PER_STEP:2 * _PAGES_PER_STEP]
    o_ref, qa_ref, pad_ref, sel_ref, m_ref, l_ref, acc_ref = refs[2 * _PAGES_PER_STEP:]
    n = pl.program_id(1)
    nsteps = pl.num_programs(1)
    nq = q_ref.shape[0]
    rows = MOBA_HEADS * nq
    prow = PAGE * MOBA_KV
    lane = lax.broadcasted_iota(jnp.int32, (rows, 128), 1)
    rowi = lax.broadcasted_iota(jnp.int32, (rows, 128), 0)
    kv_shift = (MOBA_GROUP * nq).bit_length() - 1
    nq_shift = nq.bit_length() - 1
    row_kv = rowi >> kv_shift

    @pl.when(n == 0)
    def _():
        for h in range(MOBA_HEADS):
            qa_ref[h * nq:(h + 1) * nq, :] = q_ref[:, h * MOBA_HD:(h + 1) * MOBA_HD]
        qa = qa_ref[...]
        scores = lax.dot_general(qa, km_ref[...], _NT, precision=HI, preferred_element_type=F32)
        sel_ref[...] = _topk_mask(scores, (lane & (MOBA_KV - 1)) == row_kv, lane, MOBA_TOPK)
        pad_ref[...] = jnp.zeros_like(pad_ref)
        for h in range(MOBA_KV):
            pad_ref[h * nq:(h + 1) * nq, :] = kn_ref[:, h * MOBA_HD:(h + 1) * MOBA_HD]
        logit = lax.dot_general(qa.astype(BF16), pad_ref[...].astype(BF16), _NT, preferred_element_type=F32)
        logit = jnp.where((lane & (nq - 1)) <= (rowi & (nq - 1)), logit, NEG)
        logit = jnp.where((lane >> nq_shift) == row_kv, logit, NEG)
        m0 = jnp.max(logit, axis=-1, keepdims=True)
        p0 = jnp.exp(logit - m0)
        m_ref[...] = m0
        l_ref[...] = jnp.sum(p0, axis=-1, keepdims=True)
        for h in range(MOBA_KV):
            pad_ref[h * nq:(h + 1) * nq, :] = vn_ref[:, h * MOBA_HD:(h + 1) * MOBA_HD]
        acc_ref[...] = jnp.dot(p0.astype(BF16), pad_ref[...].astype(BF16), preferred_element_type=F32)

    qa16 = qa_ref[...].astype(BF16)
    sel = sel_ref[...]
    col = lax.broadcasted_iota(jnp.int32, (rows, prow), 1)
    same_kv = (col & (MOBA_KV - 1)) == (lax.broadcasted_iota(jnp.int32, (rows, prow), 0) >> kv_shift)
    m = m_ref[...]
    l = l_ref[...]
    acc = acc_ref[...]
    for i in range(SAMPLE_BLOCKS_PER_STEP):
        blk = n * SAMPLE_BLOCKS_PER_STEP + i
        gate = jnp.max(jnp.where(lane == blk * MOBA_KV + row_kv, sel, 0.0), axis=-1, keepdims=True) > 0.5
        for half in range(2):
            kpg = kp[2 * i + half][...].astype(BF16)
            vpg = vp[2 * i + half][...].astype(BF16)
            lg = lax.dot_general(qa16, kpg, _NT, preferred_element_type=F32)
            lg = jnp.where(gate, jnp.where(same_kv, lg, NEG), NEG)
            m_new = jnp.maximum(m, jnp.max(lg, axis=-1, keepdims=True))
            alpha = jnp.exp(m - m_new)
            p = jnp.exp(lg - m_new)
            l = alpha * l + jnp.sum(p, axis=-1, keepdims=True)
            acc = alpha * acc + jnp.dot(p.astype(BF16), vpg, preferred_element_type=F32)
            m = m_new
    m_ref[...] = m
    l_ref[...] = l
    acc_ref[...] = acc

    @pl.when(n == nsteps - 1)
    def _():
        out = acc / l
        for h in range(MOBA_HEADS):
            o_ref[:, h * MOBA_HD:(h + 1) * MOBA_HD] = out[h * nq:(h + 1) * nq, :]


def _moba_sample(qn, kn, proj, cache_k, cache_v, page_table, row0, nseq, nq):
    kvw = MOBA_KV * MOBA_HD
    prow = PAGE * MOBA_KV
    n_pages = page_table.shape[1]
    nblk = n_pages * PAGE // MOBA_BLOCK
    nsteps = nblk // SAMPLE_BLOCKS_PER_STEP
    assert nblk * MOBA_KV == 128 and MOBA_BLOCK == 2 * PAGE and MOBA_KV * nq <= 128
    assert nq & (nq - 1) == 0, "new tokens per sequence must be a power of two"
    ck = cache_k.reshape(-1, prow, MOBA_HD)
    cv = cache_v.reshape(-1, prow, MOBA_HD)

    def page(k):
        return pl.BlockSpec((None, prow, MOBA_HD), lambda b, n, pt: (pt[b, n * _PAGES_PER_STEP + k], 0, 0))

    pages = [page(k) for k in range(_PAGES_PER_STEP)]
    kmean = pl.pallas_call(
        _kmean_kernel,
        grid_spec=pltpu.PrefetchScalarGridSpec(
            num_scalar_prefetch=1, grid=(nseq, nsteps),
            in_specs=pages,
            out_specs=pl.BlockSpec((SAMPLE_BLOCKS_PER_STEP * MOBA_KV, MOBA_HD),
                                   lambda b, n, pt: (b * nsteps + n, 0))),
        out_shape=jax.ShapeDtypeStruct((nseq * nblk * MOBA_KV, MOBA_HD), F32),
        compiler_params=_cp("parallel", "parallel"),
        name="moba_kmean",
    )(page_table, *([ck] * _PAGES_PER_STEP))
    rb = row0 // nq
    rows = MOBA_HEADS * nq
    return pl.pallas_call(
        _moba_sample_kernel,
        grid_spec=pltpu.PrefetchScalarGridSpec(
            num_scalar_prefetch=1, grid=(nseq, nsteps),
            in_specs=[pl.BlockSpec((nq, MOBA_HEADS * MOBA_HD), lambda b, n, pt: (rb + b, 0)),
                      pl.BlockSpec((nq, kvw), lambda b, n, pt: (rb + b, 0)),
                      pl.BlockSpec((nq, kvw), lambda b, n, pt: (rb + b, C_V // kvw)),
                      pl.BlockSpec((nblk * MOBA_KV, MOBA_HD), lambda b, n, pt: (b, 0))] + pages + pages,
            out_specs=pl.BlockSpec((nq, MOBA_HEADS * MOBA_HD), lambda b, n, pt: (b, 0)),
            scratch_shapes=[pltpu.VMEM((rows, MOBA_HD), F32),
                            pltpu.VMEM((128, MOBA_HD), F32),
                            pltpu.VMEM((rows, 128), F32),
                            pltpu.VMEM((rows, 1), F32),
                            pltpu.VMEM((rows, 1), F32),
                            pltpu.VMEM((rows, MOBA_HD), F32)]),
        out_shape=jax.ShapeDtypeStruct((nseq * nq, MOBA_HEADS * MOBA_HD), F32),
        compiler_params=_cp("parallel", "arbitrary"),
        name="moba_sample",
    )(page_table, qn, kn, proj, kmean, *([ck] * _PAGES_PER_STEP), *([cv] * _PAGES_PER_STEP))


def _split3(x):
    hi = x.astype(BF16)
    r1 = x - hi.astype(F32)
    mid = r1.astype(BF16)
    lo = (r1 - mid.astype(F32)).astype(BF16)
    return hi, mid, lo


def _dot_exact_rhs(sel, x):
    n = x.shape[1]
    y = jnp.dot(sel, jnp.concatenate(_split3(x), axis=1), preferred_element_type=F32)
    return (y[:, 0:n] + y[:, n:2 * n]) + y[:, 2 * n:3 * n]


def _dot_exact_lhs(x, sel):
    m = x.shape[0]
    y = jnp.dot(jnp.concatenate(_split3(x), axis=0), sel, preferred_element_type=F32)
    return (y[0:m] + y[m:2 * m]) + y[2 * m:3 * m]


def _ssd_kernel(*refs, q, qin, nc, has_init):
    if has_init:
        (z_ref, x_ref, b_ref, c_ref, dt_ref, wx_ref, wb_ref, wc_ref, bx_ref, bb_ref, bc_ref, par_ref, gn_ref,
         s0_ref, cx_ref, cb0_ref, cc0_ref, y_ref, so_ref, ext_ref, st_ref, tr_ref) = refs
    else:
        (z_ref, x_ref, b_ref, c_ref, dt_ref, wx_ref, wb_ref, wc_ref, bx_ref, bb_ref, bc_ref, par_ref, gn_ref,
         y_ref, so_ref, ext_ref, st_ref, tr_ref) = refs
    g = pl.program_id(1)
    c = pl.program_id(2)
    gw = SSD_GW
    ns = SSD_STATE
    cw = gw + 2 * ns

    @pl.when(c == 0)
    def _():
        if has_init:
            st_ref[...] = s0_ref[...]
            ext_ref[0:8, 0:gw] = cx_ref[...]
            ext_ref[0:8, gw:gw + ns] = cb0_ref[...]
            ext_ref[0:8, gw + ns:cw] = cc0_ref[...]
        else:
            st_ref[...] = jnp.zeros_like(st_ref)
            ext_ref[0:8, :] = jnp.zeros((8, cw), F32)
        if qin < q:
            ext_ref[8 + qin:8 + q, :] = jnp.zeros((q - qin, cw), F32)

    ext_ref[8:8 + qin, 0:gw] = x_ref[...]
    ext_ref[8:8 + qin, gw:gw + ns] = b_ref[...]
    ext_ref[8:8 + qin, gw + ns:cw] = c_ref[...]
    w = jnp.concatenate([wx_ref[...], wb_ref[...], wc_ref[...]], axis=1)
    bias = jnp.concatenate([bx_ref[...], bb_ref[...], bc_ref[...]], axis=1)
    conv = bias + ext_ref[5:5 + q, :] * w[0:1]
    conv = conv + ext_ref[6:6 + q, :] * w[1:2]
    conv = conv + ext_ref[7:7 + q, :] * w[2:3]
    conv = conv + ext_ref[8:8 + q, :] * w[3:4]
    if nc > 1:
        ext_ref[0:8, :] = ext_ref[qin:qin + 8, :]
    xbc = conv * jax.nn.sigmoid(conv)
    xs = xbc[:, 0:gw]
    bm = xbc[:, gw:gw + ns]
    cm = xbc[:, gw + ns:cw]

    par = par_ref[...]
    dt = jax.nn.softplus(dt_ref[...] + par[0:1, :])
    if qin < q:
        dt = jnp.concatenate([dt, jnp.zeros((q - qin, 128), F32)], axis=0)
    dta = dt * (-jnp.exp(par[1:2, :]))
    ri = lax.broadcasted_iota(jnp.int32, (q, q), 0)
    ci = lax.broadcasted_iota(jnp.int32, (q, q), 1)
    causal = ri >= ci
    acs = _dot_exact_rhs(causal.astype(BF16), dta)
    acs_last = acs[q - 1:q, :]
    to_end = jnp.exp(acs_last - acs) * dt
    eacs = jnp.exp(acs)
    er = lax.broadcasted_iota(jnp.int32, (128, gw), 0)
    ec = lax.broadcasted_iota(jnp.int32, (128, gw), 1)
    expand = (er == g * 8 + (ec >> 6)).astype(BF16)
    expanded = _dot_exact_lhs(
        jnp.concatenate([to_end, eacs, jnp.broadcast_to(par[2:3, :], (16, 128))], axis=0), expand)
    to_end_x = expanded[0:q]
    eacs_x = expanded[q:2 * q]
    dskip_x = expanded[2 * q:2 * q + 1]
    tr_ref[0] = acs.T
    tr_ref[1] = dt.T
    g8 = pl.multiple_of(g * 8, 8)
    acs_t = tr_ref[0, pl.ds(g8, 8), :]
    dt_t = tr_ref[1, pl.ds(g8, 8), :]
    lane = lax.broadcasted_iota(jnp.int32, (q, 128), 1)
    low = lane < SSD_HD

    cb = lax.dot_general(cm.astype(BF16), bm.astype(BF16), _NT, preferred_element_type=F32)
    ys = []
    for pr in range(4):
        ms = []
        for r in (2 * pr, 2 * pr + 1):
            col = jnp.sum(jnp.where(lane == g * 8 + r, acs, 0.0), axis=1, keepdims=True)
            seg = col - acs_t[r:r + 1, :]
            dec = jnp.where(causal, jnp.exp(jnp.minimum(seg, 0.0)), 0.0)
            ms.append((cb * dec * dt_t[r:r + 1, :]).astype(BF16))
        lhs = jnp.concatenate(ms, axis=1)
        xp = xs[:, pr * 128:(pr + 1) * 128]
        rhs = jnp.concatenate([jnp.where(low, xp, 0.0), jnp.where(low, 0.0, xp)], axis=0).astype(BF16)
        ys.append(jnp.dot(lhs, rhs, preferred_element_type=F32))
    y = jnp.concatenate(ys, axis=1)
    st = st_ref[...]
    y = y + lax.dot_general(cm.astype(BF16), st.astype(BF16), _NT, preferred_element_type=F32) * eacs_x
    y = y + dskip_x * xs
    if qin < q:
        y = y[0:qin, :]
    z = z_ref[...]
    y = y * (z * jax.nn.sigmoid(z))
    msq = jnp.mean(y * y, axis=-1, keepdims=True)
    y_ref[...] = ((y * lax.rsqrt(msq + EPS)) * gn_ref[...]).astype(y_ref.dtype)

    xw = (xs * to_end_x).astype(BF16)
    new = lax.dot_general(xw, bm.astype(BF16), _TN, preferred_element_type=F32)
    dr = lax.broadcasted_iota(jnp.int32, (gw, 128), 0)
    dc = lax.broadcasted_iota(jnp.int32, (gw, 128), 1)
    pick = (dc == g * 8 + (dr >> 6)).astype(BF16)
    last_t = jnp.exp(tr_ref[0, :, q - 1:q])
    cdec = _dot_exact_rhs(pick, jnp.broadcast_to(last_t, (128, ns)))
    st_ref[...] = st * cdec + new

    @pl.when(c == nc - 1)
    def _():
        so_ref[...] = st_ref[...]


def _ssd(proj, projdt, row0, nb, nc, q, qin, par, conv_w, conv_b, gn, out_dtype, s0=None, cst=None):
    gw, ns = SSD_GW, SSD_STATE
    rb = row0 // qin
    zb, xb = C_Z // gw, C_XBC // gw
    bb, cb = (C_XBC + D_INNER) // ns, (C_XBC + D_INNER + SSD_GROUPS * ns) // ns
    row = lambda b, g, c: rb + b * nc + c
    in_specs = [
        pl.BlockSpec((qin, gw), lambda b, g, c: (row(b, g, c), zb + g)),
        pl.BlockSpec((qin, gw), lambda b, g, c: (row(b, g, c), xb + g)),
        pl.BlockSpec((qin, ns), lambda b, g, c: (row(b, g, c), bb + g)),
        pl.BlockSpec((qin, ns), lambda b, g, c: (row(b, g, c), cb + g)),
        pl.BlockSpec((qin, 128), lambda b, g, c: (row(b, g, c), 0)),
        pl.BlockSpec((4, gw), lambda b, g, c: (0, g)),
        pl.BlockSpec((4, ns), lambda b, g, c: (0, D_INNER // ns + g)),
        pl.BlockSpec((4, ns), lambda b, g, c: (0, D_INNER // ns + SSD_GROUPS + g)),
        pl.BlockSpec((1, gw), lambda b, g, c: (0, g)),
        pl.BlockSpec((1, ns), lambda b, g, c: (0, D_INNER // ns + g)),
        pl.BlockSpec((1, ns), lambda b, g, c: (0, D_INNER // ns + SSD_GROUPS + g)),
        pl.BlockSpec((8, 128), lambda b, g, c: (0, 0)),
        pl.BlockSpec((1, gw), lambda b, g, c: (0, g)),
    ]
    args = [proj, proj, proj, proj, projdt, conv_w, conv_w, conv_w, conv_b, conv_b, conv_b, par, gn]
    has_init = s0 is not None
    if has_init:
        in_specs += [
            pl.BlockSpec((None, None, gw, ns), lambda b, g, c: (b, g, 0, 0)),
            pl.BlockSpec((None, 8, gw), lambda b, g, c: (b, 0, g)),
            pl.BlockSpec((None, 8, ns), lambda b, g, c: (b, 0, D_INNER // ns + g)),
            pl.BlockSpec((None, 8, ns), lambda b, g, c: (b, 0, D_INNER // ns + SSD_GROUPS + g)),
        ]
        args += [s0, cst, cst, cst]
    return pl.pallas_call(
        functools.partial(_ssd_kernel, q=q, qin=qin, nc=nc, has_init=has_init),
        grid=(nb, SSD_GROUPS, nc),
        in_specs=in_specs,
        out_specs=[pl.BlockSpec((qin, gw), lambda b, g, c: (b * nc + c, g)),
                   pl.BlockSpec((None, None, gw, ns), lambda b, g, c: (b, g, 0, 0))],
        out_shape=[jax.ShapeDtypeStruct((nb * nc * qin, D_INNER), out_dtype),
                   jax.ShapeDtypeStruct((nb, SSD_GROUPS, gw, ns), F32)],
        scratch_shapes=[pltpu.VMEM((q + 8, gw + 2 * ns), F32),
                        pltpu.VMEM((gw, ns), F32),
                        pltpu.VMEM((2, 128, q), F32)],
        compiler_params=_cp("parallel", "parallel", "arbitrary"),
        name="ssd",
    )(*args)


def _mem_attn_kernel(q_ref, k_ref, v_ref, o_ref):
    s = lax.dot_general(q_ref[...].astype(BF16), k_ref[...].astype(BF16), _NT, preferred_element_type=F32)
    m = jnp.max(s, axis=-1, keepdims=True)
    p = jnp.exp(s - m)
    l = jnp.sum(p, axis=-1, keepdims=True)
    o = jnp.dot(p.astype(BF16), v_ref[...].astype(BF16), preferred_element_type=F32) / l
    o_ref[...] = o.astype(o_ref.dtype)


def _mem_attn(qmn, row0, nb, seq, tq, k2d, k_cb, v2d, v_cb, out_dtype):
    nt = seq // tq
    rb = row0 // tq
    return pl.pallas_call(
        _mem_attn_kernel,
        grid=(nb, nt, MEM_HEADS),
        in_specs=[pl.BlockSpec((tq, MEM_HD), lambda b, i, h: (rb + b * nt + i, h)),
                  pl.BlockSpec((MEM_TOKENS, MEM_HD), lambda b, i, h: (b, k_cb + h)),
                  pl.BlockSpec((MEM_TOKENS, MEM_HD), lambda b, i, h: (b, v_cb + h))],
        out_specs=pl.BlockSpec((tq, MEM_HD), lambda b, i, h: (b * nt + i, h)),
        out_shape=jax.ShapeDtypeStruct((nb * seq, MEM_HEADS * MEM_HD), out_dtype),
        compiler_params=_cp("parallel", "parallel", "parallel"),
        name="mem_attn",
    )(qmn, k2d, v2d)


def _merge_kernel(a_ref, y_ref, m_ref, wa_ref, wb_ref, wc_ref, g0_ref, g1_ref, g2_ref,
                  b0_ref, b1_ref, b2_ref, o_ref):
    oa = jnp.dot(a_ref[...], wa_ref[...], preferred_element_type=F32)
    ob = jnp.dot(y_ref[...], wb_ref[...], preferred_element_type=F32)
    oc = jnp.dot(m_ref[...], wc_ref[...], preferred_element_type=F32)
    o = jax.nn.sigmoid(g0_ref[...] + b0_ref[...]) * oa
    o = o + jax.nn.sigmoid(g1_ref[...] + b1_ref[...]) * ob
    o = o + jax.nn.sigmoid(g2_ref[...] + b2_ref[...]) * oc
    o_ref[...] = o.astype(o_ref.dtype)


def _merge(attn, yssd, memo, wa, wb, wc, projb, bgate, tm=528, tn=512):
    m = attn.shape[0]
    nj = D_MODEL // tn
    gcb = CB_GATE // tn
    act = lambda k: pl.BlockSpec((tm, k), lambda i, j: (i, 0))
    wsp = lambda k: pl.BlockSpec((k, tn), lambda i, j: (0, j))
    gsp = lambda br: pl.BlockSpec((tm, tn), lambda i, j: (i, gcb + br * nj + j))
    bsp = lambda br: pl.BlockSpec((1, tn), lambda i, j: (0, br * nj + j))
    return pl.pallas_call(
        _merge_kernel,
        grid=(m // tm, nj),
        in_specs=[act(attn.shape[1]), act(yssd.shape[1]), act(memo.shape[1]),
                  wsp(wa.shape[0]), wsp(wb.shape[0]), wsp(wc.shape[0]),
                  gsp(0), gsp(1), gsp(2), bsp(0), bsp(1), bsp(2)],
        out_specs=pl.BlockSpec((tm, tn), lambda i, j: (i, j)),
        out_shape=jax.ShapeDtypeStruct((m, D_MODEL), BF16),
        compiler_params=_cp("parallel", "parallel"),
        name="merge",
    )(attn, yssd, memo, wa, wb, wc, projb, projb, projb, bgate, bgate, bgate)


def _router_kernel(x_ref, g_ref, w_ref, b_ref, h_ref, idx_ref, p_ref):
    x = x_ref[...]
    ms = jnp.mean(x * x, axis=-1, keepdims=True)
    h = (x * lax.rsqrt(ms + EPS)) * g_ref[...]
    h_ref[...] = h
    logits = jnp.dot(h, w_ref[...], precision=HI, preferred_element_type=F32) + b_ref[...]
    lane = lax.broadcasted_iota(jnp.int32, logits.shape, 1)
    lanef = lane.astype(F32)
    s = jnp.where(lane < N_EXPERTS, logits, -jnp.inf)
    vals, idxs = [], []
    for _ in range(TOP_K):
        m = jnp.max(s, axis=-1, keepdims=True)
        idx = jnp.min(jnp.where(s == m, lanef, 128.0), axis=-1, keepdims=True)
        vals.append(m)
        idxs.append(idx)
        s = jnp.where(lanef == idx, -jnp.inf, s)
    es = [jnp.exp(v - vals[0]) for v in vals]
    den = es[0] + es[1] + es[2] + es[3]
    p_out = jnp.zeros(logits.shape, F32)
    i_out = jnp.zeros(logits.shape, F32)
    for k in range(TOP_K):
        p_out = jnp.where(lane == k, es[k] / den, p_out)
        i_out = jnp.where(lane == k, idxs[k], i_out)
    p_ref[...] = p_out
    idx_ref[...] = i_out.astype(jnp.int32)


def _router(x2, g, w_router, b_router, tr=264):
    m, d = x2.shape
    wr = jnp.zeros((d, 128), F32).at[:, :N_EXPERTS].set(w_router)
    br = jnp.zeros((1, 128), F32).at[0, :N_EXPERTS].set(b_router)
    return pl.pallas_call(
        _router_kernel,
        grid=(m // tr,),
        in_specs=[pl.BlockSpec((tr, d), lambda i: (i, 0)),
                  pl.BlockSpec((1, d), lambda i: (0, 0)),
                  pl.BlockSpec((d, 128), lambda i: (0, 0)),
                  pl.BlockSpec((1, 128), lambda i: (0, 0))],
        out_specs=[pl.BlockSpec((tr, d), lambda i: (i, 0)),
                   pl.BlockSpec((tr, 128), lambda i: (i, 0)),
                   pl.BlockSpec((tr, 128), lambda i: (i, 0))],
        out_shape=[jax.ShapeDtypeStruct((m, d), F32),
                   jax.ShapeDtypeStruct((m, 128), jnp.int32),
                   jax.ShapeDtypeStruct((m, 128), F32)],
        compiler_params=_cp("parallel"),
        name="router",
    )(x2, g.reshape(1, d), wr, br)


MOE_TM = 256
MOE_TN_UP = 512
MOE_TN_DOWN = 1024
COMBINE_TOKENS = 64
DMA_UNROLL = 8
WEIGHT_PREFETCH_QUEUE = 1


def _lookup(table, idx):
    experts = jnp.arange(N_EXPERTS, dtype=jnp.int32)
    return jnp.sum(jnp.where(idx[:, None] == experts[None, :], table[None, :], 0), axis=1, dtype=jnp.int32)


def _dispatch(ridx, tm, n_tiles):
    na = ridx.shape[0] * TOP_K
    e_flat = ridx.reshape(na)
    experts = jnp.arange(N_EXPERTS, dtype=jnp.int32)
    counts = jnp.sum(e_flat[:, None] == experts[None, :], axis=0, dtype=jnp.int32)
    nt = (counts + tm - 1) // tm
    t_end = jnp.cumsum(nt)
    t0 = t_end - nt
    n_used = t_end[-1]
    start = jnp.cumsum(counts) - counts
    order = jnp.argsort(e_flat, stable=True).astype(jnp.int32)
    inv = jnp.argsort(order).astype(jnp.int32)
    pos = _lookup(t0 * tm - start, e_flat) + inv
    tiles = jnp.arange(n_tiles, dtype=jnp.int32)
    tile_e = jnp.minimum(jnp.sum(t_end[None, :] <= tiles[:, None], axis=1, dtype=jnp.int32), N_EXPERTS - 1)
    tile_t0 = _lookup(t0, tile_e)
    tile_nt = _lookup(nt, tile_e)
    k = tiles - tile_t0
    used = tiles < n_used
    tile_base = jnp.where(used, _lookup(start, tile_e) + k * tm, 0)
    tile_nv = jnp.where(used, jnp.clip(_lookup(counts, tile_e) - k * tm, 0, tm), 0)
    return order, tile_base, tile_nv, pos, (tile_e, tile_nt, tile_t0, n_used)


def _schedule(tables, n_tiles, nj):
    tile_e, tile_nt, tile_t0, n_used = tables
    s = jnp.arange(n_tiles * nj, dtype=jnp.int32)
    last = nj * n_used - 1
    valid = s <= last
    rep = lambda v: jnp.broadcast_to(v[:, None], (n_tiles, nj)).reshape(n_tiles * nj)
    e, nt, t0 = rep(tile_e), jnp.maximum(rep(tile_nt), 1), rep(tile_t0)
    local = s - nj * t0
    sj = local // nt
    si = t0 + local % nt
    first = valid & (local % nt == 0)
    nxt = jnp.minimum(s + nt, last)
    has_next = first & (s + nt <= last)
    keep = lambda v: jnp.where(valid, v, v[last])
    i32 = lambda v: v.astype(jnp.int32)
    return (keep(e), keep(sj), keep(si), i32(first), i32(valid),
            jnp.where(valid, si, s // nj), jnp.where(valid, sj, s % nj),
            e[nxt], sj[nxt], i32(has_next))


def _row_gather_pipeline(n_rows, issue_row, wait_row, step, n_steps):
    def each_row(count, fn):
        full = count // DMA_UNROLL

        def body(gi, carry):
            for u in range(DMA_UNROLL):
                fn(gi * DMA_UNROLL + u, u % 2)
            return carry

        def tail(r, carry):
            fn(r, 0)
            return carry

        lax.fori_loop(0, full, body, 0)
        lax.fori_loop(full * DMA_UNROLL, count, tail, 0)

    @pl.when(step == 0)
    def _():
        each_row(n_rows(0), lambda r, queue: issue_row(0, 0, r, queue))

    @pl.when(step + 1 < n_steps)
    def _():
        each_row(n_rows(step + 1), lambda r, queue: issue_row(step + 1, (step + 1) & 1, r, queue))

    each_row(n_rows(step), lambda r, queue: wait_row(step & 1, r))


def _gather_kernel(order_ref, base_ref, nv_ref, h_hbm, o_ref, buf_ref, sem):
    i = pl.program_id(0)

    @pl.when(i == 0)
    def _():
        buf_ref[...] = jnp.zeros_like(buf_ref)

    def copy(slot, r, src_row):
        return pltpu.make_async_copy(h_hbm.at[pl.ds(src_row, 1), :], buf_ref.at[slot, pl.ds(r, 1), :],
                                     sem.at[slot])

    _row_gather_pipeline(
        lambda st: nv_ref[st],
        lambda st, slot, r, queue: copy(slot, r, order_ref[base_ref[st] + r] // TOP_K).start(priority=queue),
        lambda slot, r: copy(slot, r, 0).wait(),
        i, pl.num_programs(0))
    o_ref[...] = buf_ref[i & 1].astype(o_ref.dtype)


def _gather_rows(order, tile_base, tile_nv, h2, rows):
    n_tiles = tile_base.shape[0]
    d = h2.shape[1]
    return pl.pallas_call(
        _gather_kernel,
        grid_spec=pltpu.PrefetchScalarGridSpec(
            num_scalar_prefetch=3, grid=(n_tiles,),
            in_specs=[pl.BlockSpec(memory_space=pl.ANY)],
            out_specs=pl.BlockSpec((rows, d), lambda i, *_: (i, 0)),
            scratch_shapes=[pltpu.VMEM((2, rows, d), F32), pltpu.SemaphoreType.DMA((2,))]),
        out_shape=jax.ShapeDtypeStruct((n_tiles * rows, d), BF16),
        compiler_params=_cp("arbitrary"),
        name="moe_gather",
    )(order, tile_base, tile_nv, h2)


def _weight_block_copies(w_hbms, bufs, sem, e, j):
    tn = bufs[0].shape[1]
    c0 = pl.multiple_of(j * tn, tn)
    return [pltpu.make_async_copy(w.at[e, :, pl.ds(c0, tn)], b, sem.at[k])
            for k, (w, b) in enumerate(zip(w_hbms, bufs))]


def _stage_weights(s, se, sj, ne, nj, has_next, w_hbms, bufs, w16s, sem):
    @pl.when(s == 0)
    def _():
        for c in _weight_block_copies(w_hbms, bufs, sem, se[0], sj[0]):
            c.start()

    for c in _weight_block_copies(w_hbms, bufs, sem, se[s], sj[s]):
        c.wait()
    for b, w16 in zip(bufs, w16s):
        w16[...] = b[...].astype(BF16)

    @pl.when(has_next[s] == 1)
    def _():
        for c in _weight_block_copies(w_hbms, bufs, sem, ne[s], nj[s]):
            c.start(priority=WEIGHT_PREFETCH_QUEUE)


def _moe_up_kernel(se, sj, si, sfirst, svalid, so_i, so_j, ne, nj, has_next,
                   x_ref, wg_hbm, wu_hbm, bg_ref, bu_ref, o_ref, wgf, wuf, wg16, wu16, sem):
    s = pl.program_id(0)

    @pl.when(svalid[s] == 1)
    def _():
        @pl.when(sfirst[s] == 1)
        def _():
            _stage_weights(s, se, sj, ne, nj, has_next, (wg_hbm, wu_hbm), (wgf, wuf), (wg16, wu16), sem)

        x = x_ref[...]
        g = jnp.minimum(jnp.dot(x, wg16[...], preferred_element_type=F32) + bg_ref[...], SWIGLU_LIMIT)
        u = jnp.clip(jnp.dot(x, wu16[...], preferred_element_type=F32) + bu_ref[...], -SWIGLU_LIMIT, SWIGLU_LIMIT)
        act = (u + 1.0) * (g * jax.nn.sigmoid(SWIGLU_ALPHA * g))
        o_ref[...] = act.astype(o_ref.dtype)

    @pl.when(svalid[s] == 0)
    def _():
        o_ref[...] = jnp.zeros_like(o_ref)


def _moe_down_kernel(se, sj, si, sfirst, svalid, so_i, so_j, ne, nj, has_next,
                     a_ref, w_hbm, b_ref, o_ref, wf, w16, sem):
    s = pl.program_id(0)

    @pl.when(svalid[s] == 1)
    def _():
        @pl.when(sfirst[s] == 1)
        def _():
            _stage_weights(s, se, sj, ne, nj, has_next, (w_hbm,), (wf,), (w16,), sem)

        o_ref[...] = jnp.dot(a_ref[...], w16[...], preferred_element_type=F32) + b_ref[...]

    @pl.when(svalid[s] == 0)
    def _():
        o_ref[...] = jnp.zeros_like(o_ref)


def _moe_up(sched, xs, wg, wu, bg, bu, tm, tn):
    rows, d = xs.shape
    f = wg.shape[2]
    nsteps = sched[0].shape[0]
    hbm = pl.BlockSpec(memory_space=pl.ANY)
    bsp = pl.BlockSpec((None, 1, tn), lambda s, se, sj, *_: (se[s], 0, sj[s]))
    return pl.pallas_call(
        _moe_up_kernel,
        grid_spec=pltpu.PrefetchScalarGridSpec(
            num_scalar_prefetch=len(sched), grid=(nsteps,),
            in_specs=[pl.BlockSpec((tm, d), lambda s, se, sj, si, *_: (si[s], 0)), hbm, hbm, bsp, bsp],
            out_specs=pl.BlockSpec((tm, tn), lambda s, se, sj, si, sf, sv, oi, oj, *_: (oi[s], oj[s])),
            scratch_shapes=[pltpu.VMEM((d, tn), F32), pltpu.VMEM((d, tn), F32),
                            pltpu.VMEM((d, tn), BF16), pltpu.VMEM((d, tn), BF16),
                            pltpu.SemaphoreType.DMA((2,))]),
        out_shape=jax.ShapeDtypeStruct((rows, f), BF16),
        compiler_params=_cp("arbitrary"),
        name="moe_up",
    )(*sched, xs, wg, wu, bg, bu)


def _moe_down(sched, act, wd, bd, tm, tn):
    rows, f = act.shape
    d = wd.shape[2]
    nsteps = sched[0].shape[0]
    return pl.pallas_call(
        _moe_down_kernel,
        grid_spec=pltpu.PrefetchScalarGridSpec(
            num_scalar_prefetch=len(sched), grid=(nsteps,),
            in_specs=[pl.BlockSpec((tm, f), lambda s, se, sj, si, *_: (si[s], 0)),
                      pl.BlockSpec(memory_space=pl.ANY),
                      pl.BlockSpec((None, 1, tn), lambda s, se, sj, *_: (se[s], 0, sj[s]))],
            out_specs=pl.BlockSpec((tm, tn), lambda s, se, sj, si, sf, sv, oi, oj, *_: (oi[s], oj[s])),
            scratch_shapes=[pltpu.VMEM((f, tn), F32), pltpu.VMEM((f, tn), BF16),
                            pltpu.SemaphoreType.DMA((1,))]),
        out_shape=jax.ShapeDtypeStruct((rows, d), F32),
        compiler_params=_cp("arbitrary"),
        name="moe_down",
    )(*sched, act, wd, bd)


def _combine_kernel(pos_ref, x_ref, p_ref, y_hbm, o_ref, buf_ref, sem):
    i = pl.program_id(0)
    n = COMBINE_TOKENS * TOP_K

    def copy(slot, r, src_row):
        return pltpu.make_async_copy(y_hbm.at[pl.ds(src_row, 1), :],
                                     buf_ref.at[slot, r & (TOP_K - 1), pl.ds(r >> 2, 1), :], sem.at[slot])

    _row_gather_pipeline(
        lambda st: n,
        lambda st, slot, r, queue: copy(slot, r, pos_ref[st * n + r]).start(priority=queue),
        lambda slot, r: copy(slot, r, 0).wait(),
        i, pl.num_programs(0))
    slot = i & 1
    p = p_ref[...]
    acc = x_ref[...]
    for k in range(TOP_K):
        acc = acc + p[:, k:k + 1] * buf_ref[slot, k]
    o_ref[...] = acc


def _combine(pos, x2, rprob, ysort):
    t, d = x2.shape
    assert TOP_K == 4
    return pl.pallas_call(
        _combine_kernel,
        grid_spec=pltpu.PrefetchScalarGridSpec(
            num_scalar_prefetch=1, grid=(t // COMBINE_TOKENS,),
            in_specs=[pl.BlockSpec((COMBINE_TOKENS, d), lambda i, pos: (i, 0)),
                      pl.BlockSpec((COMBINE_TOKENS, 128), lambda i, pos: (i, 0)),
                      pl.BlockSpec(memory_space=pl.ANY)],
            out_specs=pl.BlockSpec((COMBINE_TOKENS, d), lambda i, pos: (i, 0)),
            scratch_shapes=[pltpu.VMEM((2, TOP_K, COMBINE_TOKENS, d), F32), pltpu.SemaphoreType.DMA((2,))]),
        out_shape=jax.ShapeDtypeStruct((t, d), F32),
        compiler_params=_cp("arbitrary"),
        name="moe_combine",
    )(pos, x2, rprob, ysort)


def _moe(x2, norm_g, w_router, b_router, wg, bg, wu, bu, wd, bd):
    t = x2.shape[0]
    h2, ridx, rprob = _router(x2, norm_g, w_router, b_router)
    n_tiles = (t * TOP_K) // MOE_TM + N_EXPERTS
    order, tile_base, tile_nv, pos, tables = _dispatch(ridx[:, :TOP_K], MOE_TM, n_tiles)
    xs = _gather_rows(order, tile_base, tile_nv, h2, MOE_TM)
    act = _moe_up(_schedule(tables, n_tiles, D_FF // MOE_TN_UP), xs, wg, wu,
                  bg.reshape(N_EXPERTS, 1, D_FF), bu.reshape(N_EXPERTS, 1, D_FF), MOE_TM, MOE_TN_UP)
    ysort = _moe_down(_schedule(tables, n_tiles, D_MODEL // MOE_TN_DOWN), act, wd,
                      bd.reshape(N_EXPERTS, 1, D_MODEL), MOE_TM, MOE_TN_DOWN)
    return _combine(pos, x2, rprob, ysort)


def kernel(x_prompt, x_sample, mem_prompt, cache_k, cache_v, page_table, state_ssm, state_conv, cache_mem_k, cache_mem_v, norm_mix_g, w_in, b_branch_gate, moba_q_norm_g, moba_k_norm_g, conv_w, conv_b, dt_bias, a_log, d_skip, ssd_norm_g, mem_norm_g, w_mem_kv, mem_q_norm_g, mem_k_norm_g, w_br_moba, w_br_ssd, w_br_mem, w_out, norm_ffn_g, w_router, b_router, w_exp_gate, b_exp_gate, w_exp_up, b_exp_up, w_exp_down, b_exp_down):
    assert w_in.shape[0] == 1, "single layer"
    nb, seq, d = x_prompt.shape
    ns, nq, _ = x_sample.shape
    tp = nb * seq
    x_all = jnp.concatenate([x_prompt.reshape(tp, d), x_sample.reshape(ns * nq, d)], axis=0)

    w_in2 = w_in.reshape(d, -1)
    h = _rmsnorm(x_all, norm_mix_g[0], BF16)
    w_in_t = w_in2.T
    n_in = w_in_t.shape[0]
    proja = _in_proj(h, w_in_t, 0, W_IN_A, tm=1056, tn=512, name="in_proj_a")
    projdt = _in_proj(h, w_in_t, W_IN_A, W_IN_B - W_IN_A, tm=1056, tn=128, name="in_proj_dt")
    projb = _in_proj(h, w_in_t, W_IN_B, n_in - W_IN_B, tm=1056, tn=512, name="in_proj_b")

    qn = _head_norm(proja, C_Q, MOBA_HEADS * MOBA_HD, MOBA_HD, moba_q_norm_g[0], MOBA_HD ** -0.5)
    kn = _head_norm(proja, C_K, MOBA_KV * MOBA_HD, MOBA_HD, moba_k_norm_g[0], 1.0)
    qmn = _head_norm(projb, CB_QM, MEM_HEADS * MEM_HD, MEM_HD, mem_q_norm_g[0], MEM_HD ** -0.5)

    attn_p = _moba_prompt(qn, kn, proja, nb, seq)
    attn_s = _moba_sample(qn, kn, proja, cache_k, cache_v, page_table, tp, ns, nq)
    attn = jnp.concatenate([attn_p, attn_s.astype(BF16)], axis=0)

    par = jnp.zeros((8, 128), F32)
    par = par.at[0, :SSD_HEADS].set(dt_bias[0]).at[1, :SSD_HEADS].set(a_log[0]).at[2, :SSD_HEADS].set(d_skip[0])
    gn = ssd_norm_g.reshape(1, D_INNER)
    cw2 = conv_w.reshape(-1, CONV_CH)
    cb2 = conv_b.reshape(1, CONV_CH)
    y_p, st_p = _ssd(proja, projdt, 0, nb, seq // 256, 256, 256, par, cw2, cb2, gn, BF16)
    s0 = state_ssm.reshape(ns, SSD_GROUPS, SSD_GW, SSD_STATE)
    cst = jnp.concatenate([jnp.zeros((ns, 5, CONV_CH), F32), state_conv.reshape(ns, 3, CONV_CH)], axis=1)
    y_s, st_s = _ssd(proja, projdt, tp, ns, 1, 128, nq, par, cw2, cb2, gn, F32, s0=s0, cst=cst)
    yssd = jnp.concatenate([y_p, y_s.astype(BF16)], axis=0)

    hm = _rmsnorm(mem_prompt.reshape(nb * MEM_TOKENS, d), mem_norm_g[0], BF16)
    memkv = _matmul(hm, w_mem_kv.reshape(d, -1).astype(BF16), tm=512, tn=512, name="mem_kv")
    mw = MEM_HEADS * MEM_HD
    mk = _head_norm(memkv, 0, mw, MEM_HD, mem_k_norm_g[0], 1.0, tr=256)
    memo_p = _mem_attn(qmn, 0, nb, seq, 256, mk, 0, memkv, MEM_HEADS, BF16)
    cmk = cache_mem_k.reshape(ns * MEM_TOKENS, mw)
    cmv = cache_mem_v.reshape(ns * MEM_TOKENS, mw)
    memo_s = _mem_attn(qmn, tp, ns, nq, nq, cmk, 0, cmv, 0, F32)
    memo = jnp.concatenate([memo_p, memo_s.astype(BF16)], axis=0)

    merged = _merge(attn, yssd, memo, w_br_moba.reshape(-1, d).astype(BF16), w_br_ssd.reshape(-1, d).astype(BF16),
                    w_br_mem.reshape(-1, d).astype(BF16), projb, b_branch_gate.reshape(1, -1))
    x2 = _matmul(merged, w_out.reshape(d, d).astype(BF16), tm=1056, tn=512, residual=x_all, name="out_proj")

    y = _moe(x2, norm_ffn_g[0], w_router.reshape(d, N_EXPERTS), b_router[0],
             w_exp_gate.reshape(N_EXPERTS, d, D_FF), b_exp_gate.reshape(N_EXPERTS, D_FF),
             w_exp_up.reshape(N_EXPERTS, d, D_FF), b_exp_up.reshape(N_EXPERTS, D_FF),
             w_exp_down.reshape(N_EXPERTS, D_FF, d), b_exp_down.reshape(N_EXPERTS, d))

    kvw = MOBA_KV * MOBA_HD
    v_all = proja[:, C_V:C_V + kvw]
    conv_p = jnp.stack([proja[(b + 1) * seq - 3:(b + 1) * seq, C_XBC:C_XBC + CONV_CH] for b in range(nb)])
    conv_s = proja[tp:, C_XBC:C_XBC + CONV_CH].reshape(ns, nq, CONV_CH)[:, nq - 3:]
    return (
        y[:tp].reshape(nb, seq, d),
        y[tp:].reshape(ns, nq, d),
        kn[:tp].reshape(1, nb, seq, MOBA_KV, MOBA_HD),
        v_all[:tp].reshape(1, nb, seq, MOBA_KV, MOBA_HD),
        kn[tp:].reshape(1, ns, nq, MOBA_KV, MOBA_HD),
        v_all[tp:].reshape(1, ns, nq, MOBA_KV, MOBA_HD),
        st_p.reshape(1, nb, SSD_HEADS, SSD_HD, SSD_STATE),
        conv_p[None],
        st_s.reshape(1, ns, SSD_HEADS, SSD_HD, SSD_STATE),
        conv_s[None],
        mk.reshape(1, nb, MEM_TOKENS, MEM_HEADS, MEM_HD),
        memkv[:, mw:].reshape(1, nb, MEM_TOKENS, MEM_HEADS, MEM_HD),
    )
```
